```python
import jax, jax.numpy as jnp
from jax import lax
import numpy as np

D_MODEL = 1024
BATCH = 8
SEQ = 4096
DEPTH = 1

RET_HEADS = 4
RET_DK = 128
RET_DV = 128
RET_CHUNK = 128
RET_W = RET_HEADS * RET_DK
RET_VW = RET_HEADS * RET_DV
ATT_Q_HEADS = 8
ATT_KV_HEADS = 2
ATT_HD = 64
ATT_QW = ATT_Q_HEADS * ATT_HD
ATT_KVW = ATT_KV_HEADS * ATT_HD
WINDOW = 128
ATT_BLOCK = 128
ROPE_THETA = 10000.0
D_FF = 2752
EPS = 1e-6
IN_SIZES = (RET_W, RET_W, RET_VW, RET_VW, ATT_QW, ATT_KVW, ATT_KVW, D_MODEL, D_MODEL)
IN_COLS = sum(IN_SIZES)
IN_SPLITS = tuple(int(v) for v in np.cumsum(IN_SIZES)[:-1])

kernel_name = "hybrid_retention_swa_macaron_block"


def rmsnorm(x, g):
    xf = x.astype(jnp.float32)
    y = xf * lax.rsqrt(jnp.mean(xf * xf, axis=-1, keepdims=True) + EPS)
    return (y * g.astype(jnp.float32)).astype(x.dtype)


def rope(x):
    s, d = x.shape[1], x.shape[-1]
    half = d // 2
    inv_freq = ROPE_THETA ** (-jnp.arange(half, dtype=jnp.float32) / half)
    ang = jnp.arange(s, dtype=jnp.float32)[:, None] * inv_freq[None, :]
    cos = jnp.cos(ang)[None, :, None, :].astype(x.dtype)
    sin = jnp.sin(ang)[None, :, None, :].astype(x.dtype)
    x1, x2 = x[..., :half], x[..., half:]
    return jnp.concatenate([x1 * cos - x2 * sin, x1 * sin + x2 * cos], axis=-1)


def swiglu(x, w_in, w_out):
    gu = x @ w_in
    g, u = jnp.split(gu, 2, axis=-1)
    return (jax.nn.silu(g) * u) @ w_out


def retention_dir(q, k, v, log_gamma, include_diag):
    b, h, s, dk = q.shape
    dv = v.shape[-1]
    c = RET_CHUNK
    nc = s // c
    lg = log_gamma.astype(jnp.float32)
    idx = jnp.arange(c, dtype=jnp.float32)
    rel = idx[:, None] - idx[None, :]
    mask = (rel >= 0) if include_diag else (rel > 0)
    d_in = jnp.where(mask[None], jnp.exp(jnp.maximum(rel, 0.0)[None] * lg[:, None, None]), 0.0).astype(q.dtype)
    qc = q.reshape(b, h, nc, c, dk)
    kc = k.reshape(b, h, nc, c, dk)
    vc = v.reshape(b, h, nc, c, dv)
    scores = jnp.einsum('bhncd,bhnkd->bhnck', qc, kc) * d_in[None, :, None]
    inner = jnp.einsum('bhnck,bhnke->bhnce', scores, vc)
    k_dec = jnp.exp((c - 1 - idx)[None, :] * lg[:, None]).astype(q.dtype)
    kv = jnp.einsum('bhnkd,bhnke->nbhde', kc * k_dec[None, :, None, :, None], vc)
    g_c = jnp.exp(c * lg).astype(kv.dtype)[:, None, None]

    def step(state, kv_n):
        return g_c * state + kv_n, state

    _, states = lax.scan(step, jnp.zeros_like(kv[0]), kv)
    q_dec = jnp.exp((idx + 1.0)[None, :] * lg[:, None]).astype(q.dtype)
    cross = jnp.einsum('bhncd,nbhde->bhnce', qc * q_dec[None, :, None, :, None], states)
    return (inner + cross).reshape(b, h, s, dv)


def retention_branch(q, k, v, g, decay_fwd, decay_bwd, gn_gain, w_o):
    b, s = q.shape[0], q.shape[1]
    q = jnp.transpose(rope(q), (0, 2, 1, 3))
    k = jnp.transpose(rope(k) * (RET_DK ** -0.5), (0, 2, 1, 3))
    v = jnp.transpose(v, (0, 2, 1, 3))
    lg_f = -jnp.exp(decay_fwd.astype(jnp.float32))
    lg_b = -jnp.exp(decay_bwd.astype(jnp.float32))
    y_f = retention_dir(q, k, v, lg_f, True)
    y_b = retention_dir(q[:, :, ::-1], k[:, :, ::-1], v[:, :, ::-1], lg_b, False)[:, :, ::-1]
    y = (y_f + y_b).astype(jnp.float32)
    mu = jnp.mean(y, axis=-1, keepdims=True)
    var = jnp.mean(jnp.square(y - mu), axis=-1, keepdims=True)
    y = (y - mu) * lax.rsqrt(var + EPS)
    y = jnp.transpose(y, (0, 2, 1, 3)).reshape(b, s, RET_VW) * gn_gain.astype(jnp.float32)
    y = y.astype(g.dtype) * jax.nn.silu(g)
    return y @ w_o


def window_attention_branch(q, k, v, sink, w_o):
    b, s = q.shape[0], q.shape[1]
    nb = s // ATT_BLOCK
    grp = ATT_Q_HEADS // ATT_KV_HEADS
    q = rope(q)
    k = rope(k)
    qb = q.reshape(b, nb, ATT_BLOCK, ATT_KV_HEADS, grp, ATT_HD)
    pad = ((0, 0), (ATT_BLOCK, ATT_BLOCK), (0, 0), (0, 0))
    kp = jnp.pad(k, pad).reshape(b, nb + 2, ATT_BLOCK, ATT_KV_HEADS, ATT_HD)
    vp = jnp.pad(v, pad).reshape(b, nb + 2, ATT_BLOCK, ATT_KV_HEADS, ATT_HD)
    kb = jnp.concatenate([kp[:, :-2], kp[:, 1:-1], kp[:, 2:]], axis=2)
    vb = jnp.concatenate([vp[:, :-2], vp[:, 1:-1], vp[:, 2:]], axis=2)
    sc = jnp.einsum('bnqhgd,bnkhd->bnhgqk', qb, kb).astype(jnp.float32) * (ATT_HD ** -0.5)
    blk = jnp.arange(nb)
    qpos = blk[:, None] * ATT_BLOCK + jnp.arange(ATT_BLOCK)[None, :]
    kpos = (blk[:, None] - 1) * ATT_BLOCK + jnp.arange(3 * ATT_BLOCK)[None, :]
    valid = (jnp.abs(qpos[:, :, None] - kpos[:, None, :]) <= WINDOW) \
        & (kpos >= 0)[:, None, :] & (kpos < s)[:, None, :]
    sc = jnp.where(valid[None, :, None, None], sc, -jnp.inf)
    snk = sink.astype(jnp.float32).reshape(1, 1, ATT_KV_HEADS, grp, 1, 1)
    m = jnp.maximum(jnp.max(sc, axis=-1, keepdims=True), snk)
    p = jnp.exp(sc - m)
    p = p / (jnp.sum(p, axis=-1, keepdims=True) + jnp.exp(snk - m))
    o = jnp.einsum('bnhgqk,bnkhd->bnqhgd', p.astype(vb.dtype), vb).reshape(b, s, ATT_QW)
    return o @ w_o


def setup_inputs(seed: int = 0) -> dict:
    key = jax.random.key(seed)
    ks = jax.random.split(key, 20)

    def w(k, shape, fan_in):
        return jax.random.normal(k, shape, jnp.float32) * (fan_in ** -0.5)

    def gain(k, n):
        return 1.0 + 0.02 * jax.random.normal(k, (DEPTH, n), jnp.float32)

    base_decay = np.log(-np.log(1.0 - 2.0 ** (-5.0 - np.arange(RET_HEADS)))).astype(np.float32)
    x = jax.random.normal(ks[0], (BATCH, SEQ, D_MODEL), jnp.float32)
    return {
        "x": x,
        "norm_ffn1": gain(ks[1], D_MODEL),
        "ffn1_w_in": w(ks[2], (DEPTH, D_MODEL, 2 * D_FF), D_MODEL),
        "ffn1_w_out": w(ks[3], (DEPTH, D_FF, D_MODEL), D_FF),
        "norm_mix": gain(ks[4], D_MODEL),
        "w_in": w(ks[5], (DEPTH, D_MODEL, IN_COLS), D_MODEL),
        "ret_decay_fwd": jnp.asarray(base_decay)[None] + 0.05 * jax.random.normal(ks[6], (DEPTH, RET_HEADS), jnp.float32),
        "ret_decay_bwd": jnp.asarray(base_decay)[None] + 0.05 * jax.random.normal(ks[7], (DEPTH, RET_HEADS), jnp.float32),
        "ret_gn_gain": gain(ks[8], RET_VW),
        "w_ret_out": w(ks[9], (DEPTH, RET_VW, D_MODEL), RET_VW),
        "att_sink": 0.5 * jax.random.normal(ks[10], (DEPTH, ATT_Q_HEADS), jnp.float32),
        "w_att_out": w(ks[11], (DEPTH, ATT_QW, D_MODEL), ATT_QW),
        "w_out": w(ks[12], (DEPTH, D_MODEL, D_MODEL), D_MODEL),
        "norm_ffn2": gain(ks[13], D_MODEL),
        "ffn2_w_in": w(ks[14], (DEPTH, D_MODEL, 2 * D_FF), D_MODEL),
        "ffn2_w_out": w(ks[15], (DEPTH, D_FF, D_MODEL), D_FF),
        "norm_final": 1.0 + 0.02 * jax.random.normal(ks[16], (D_MODEL,), jnp.float32),
    }


def reference(x, norm_ffn1, ffn1_w_in, ffn1_w_out, norm_mix, w_in, ret_decay_fwd, ret_decay_bwd,
              ret_gn_gain, w_ret_out, att_sink, w_att_out, w_out, norm_ffn2, ffn2_w_in, ffn2_w_out,
              norm_final):
    b, s, _ = x.shape
    for l in range(DEPTH):
        x = x + 0.5 * swiglu(rmsnorm(x, norm_ffn1[l]), ffn1_w_in[l], ffn1_w_out[l])
        h = rmsnorm(x, norm_mix[l])
        proj = h @ w_in[l]
        q_r, k_r, v_r, g_r, q_a, k_a, v_a, gate_r, gate_a = jnp.split(proj, IN_SPLITS, axis=-1)
        y_ret = retention_branch(
            q_r.reshape(b, s, RET_HEADS, RET_DK), k_r.reshape(b, s, RET_HEADS, RET_DK),
            v_r.reshape(b, s, RET_HEADS, RET_DV), g_r,
            ret_decay_fwd[l], ret_decay_bwd[l], ret_gn_gain[l], w_ret_out[l])
        y_att = window_attention_branch(
            q_a.reshape(b, s, ATT_Q_HEADS, ATT_HD), k_a.reshape(b, s, ATT_KV_HEADS, ATT_HD),
            v_a.reshape(b, s, ATT_KV_HEADS, ATT_HD), att_sink[l], w_att_out[l])
        merged = jax.nn.sigmoid(gate_r) * y_ret + jax.nn.sigmoid(gate_a) * y_att
        x = x + merged @ w_out[l]
        x = x + 0.5 * swiglu(rmsnorm(x, norm_ffn2[l]), ffn2_w_in[l], ffn2_w_out[l])
    return rmsnorm(x, norm_final)
```

```python
import functools

import jax
import jax.numpy as jnp
import numpy as np
from jax import lax
from jax.experimental import pallas as pl
from jax.experimental.pallas import tpu as pltpu

F32 = jnp.float32
BF16 = jnp.bfloat16

EPS = 1e-6
ROPE_THETA = 10000.0
RET_HEADS = 4
RET_DK = 128
RET_CHUNK = 128
ATT_Q_HEADS = 8
ATT_KV_HEADS = 2
ATT_HD = 64
ATT_BLOCK = 128
LANES = 128
FF_CHUNK = 256
TOKEN_TILE = 512
VMEM_LIMIT = 56 * 1024 * 1024


def _cparams(n_grid):
    return pltpu.CompilerParams(dimension_semantics=("arbitrary",) * n_grid,
                                vmem_limit_bytes=VMEM_LIMIT)


def _resident(shape):
    zeros = (0,) * len(shape)
    return pl.BlockSpec(shape, lambda *_: zeros, pipeline_mode=pl.Buffered(1))


def _rms(x, g):
    y = x * lax.rsqrt(jnp.mean(x * x, axis=-1, keepdims=True) + EPS)
    return y * g


def _dot(a, b):
    return jnp.dot(a, b, preferred_element_type=F32)


def _dot_nt(a, b):
    return lax.dot_general(a, b, (((1,), (1,)), ((), ())), preferred_element_type=F32)


def _swiglu_residual(x, gn_ref, wgu_ref, wo_ref, h_ref, a_ref):
    h_ref[...] = _rms(x, gn_ref[...]).astype(BF16)
    n_chunks = wgu_ref.shape[0]
    fc = wgu_ref.shape[2] // 2
    for c in range(n_chunks):
        gu = _dot(h_ref[...], wgu_ref[c])
        g = gu[:, :fc]
        u = gu[:, fc:]
        a_ref[:, c * fc:(c + 1) * fc] = (jax.nn.silu(g) * u).astype(BF16)
    return x + 0.5 * _dot(a_ref[...], wo_ref[...])


def _ffn_kernel(x_ref, gn_ref, wgu_ref, wo_ref, o_ref, h_ref, a_ref):
    o_ref[...] = _swiglu_residual(x_ref[...], gn_ref, wgu_ref, wo_ref, h_ref, a_ref)


def _ffn_call(x, gn, wgu, wo):
    t, d = x.shape
    tm = TOKEN_TILE
    ffp = wo.shape[0]
    return pl.pallas_call(
        _ffn_kernel,
        grid=(t // tm,),
        in_specs=[pl.BlockSpec((tm, d), lambda i: (i, 0)),
                  _resident(gn.shape), _resident(wgu.shape), _resident(wo.shape)],
        out_specs=pl.BlockSpec((tm, d), lambda i: (i, 0)),
        out_shape=jax.ShapeDtypeStruct((t, d), F32),
        scratch_shapes=[pltpu.VMEM((tm, d), BF16), pltpu.VMEM((tm, ffp), BF16)],
        compiler_params=_cparams(1),
        name="ffn1",
    )(x, gn, wgu, wo)


def _rope128(y, cos, sin_signed):
    return y * cos + pltpu.roll(y, 64, 1) * sin_signed


def _proj_kernel(x_ref, gn_ref, w_ref, cosr_ref, sinr_ref, cosa_ref, sina_ref,
                 qr_ref, kr_ref, vr_ref, sr_ref, qa_ref, k4_ref, v4_ref, sgr_ref, sga_ref,
                 h_ref, *, ret_w, att_qw):
    h_ref[...] = _rms(x_ref[...], gn_ref[...]).astype(BF16)

    def proj(a, b):
        return _dot(h_ref[...], w_ref[:, a:b])

    cosr, sinr = cosr_ref[...], sinr_ref[...]
    cosa, sina = cosa_ref[...], sina_ref[...]
    c0 = 0
    y = proj(c0, c0 + ret_w)
    for j in range(ret_w // LANES):
        sl = slice(j * LANES, (j + 1) * LANES)
        qr_ref[:, sl] = _rope128(y[:, sl], cosr, sinr).astype(BF16)
    c0 += ret_w
    y = proj(c0, c0 + ret_w)
    for j in range(ret_w // LANES):
        sl = slice(j * LANES, (j + 1) * LANES)
        kr_ref[:, sl] = (_rope128(y[:, sl], cosr, sinr) * (RET_DK ** -0.5)).astype(BF16)
    c0 += ret_w
    vr_ref[...] = proj(c0, c0 + ret_w).astype(BF16)
    c0 += ret_w
    sr_ref[...] = jax.nn.silu(proj(c0, c0 + ret_w)).astype(BF16)
    c0 += ret_w
    y = proj(c0, c0 + att_qw)
    for j in range(att_qw // LANES):
        sl = slice(j * LANES, (j + 1) * LANES)
        qa_ref[:, sl] = (_rope128(y[:, sl], cosa, sina) * (ATT_HD ** -0.5)).astype(BF16)
    c0 += att_qw
    y = proj(c0, c0 + 2 * LANES)
    kk = _rope128(y[:, :LANES], cosa, sina)
    vv = y[:, LANES:]
    lane = lax.broadcasted_iota(jnp.int32, kk.shape, 1)
    even = (lane // 32) % 2 == 0
    low = lane < 64
    k4_ref[:, 0 * LANES:1 * LANES] = jnp.where(even, kk, 0.0).astype(BF16)
    k4_ref[:, 1 * LANES:2 * LANES] = jnp.where(even, 0.0, pltpu.roll(kk, 32, 1)).astype(BF16)
    k4_ref[:, 2 * LANES:3 * LANES] = jnp.where(even, pltpu.roll(kk, 96, 1), 0.0).astype(BF16)
    k4_ref[:, 3 * LANES:4 * LANES] = jnp.where(even, 0.0, kk).astype(BF16)
    vroll = pltpu.roll(vv, 64, 1)
    v4_ref[:, 0 * LANES:1 * LANES] = jnp.where(low, vv, 0.0).astype(BF16)
    v4_ref[:, 1 * LANES:2 * LANES] = jnp.where(low, 0.0, vroll).astype(BF16)
    v4_ref[:, 2 * LANES:3 * LANES] = jnp.where(low, vroll, 0.0).astype(BF16)
    v4_ref[:, 3 * LANES:4 * LANES] = jnp.where(low, 0.0, vv).astype(BF16)
    c0 += 2 * LANES
    d = x_ref.shape[1]
    sgr_ref[...] = jax.nn.sigmoid(proj(c0, c0 + d)).astype(BF16)
    c0 += d
    sga_ref[...] = jax.nn.sigmoid(proj(c0, c0 + d)).astype(BF16)


def _proj_call(x1, gn, w, cosr, sinr, cosa, sina, seq):
    t, d = x1.shape
    tm = TOKEN_TILE
    ret_w = RET_HEADS * RET_DK
    att_qw = ATT_Q_HEADS * ATT_HD
    n_pos = seq // tm
    tok = lambda width: pl.BlockSpec((tm, width), lambda i: (i, 0))
    tab = pl.BlockSpec((tm, LANES), lambda i: (i % n_pos, 0))
    widths = (ret_w, ret_w, ret_w, ret_w, att_qw, 4 * LANES, 4 * LANES, d, d)
    return pl.pallas_call(
        functools.partial(_proj_kernel, ret_w=ret_w, att_qw=att_qw),
        grid=(t // tm,),
        in_specs=[tok(d), _resident(gn.shape), _resident(w.shape), tab, tab, tab, tab],
        out_specs=[tok(wd) for wd in widths],
        out_shape=[jax.ShapeDtypeStruct((t, wd), BF16) for wd in widths],
        scratch_shapes=[pltpu.VMEM((tm, d), BF16)],
        compiler_params=_cparams(1),
        name="proj",
    )(x1, gn, w, cosr, sinr, cosa, sina)


def _ret_kernel(dec_ref, q_ref, k_ref, v_ref, s_ref, gain_ref, z_ref,
                kvf_ref, kvb_ref, st_ref):
    c = RET_CHUNK
    n_chunks = q_ref.shape[0] // c
    h = pl.program_id(1)
    lg_f = -jnp.exp(jnp.full((c, c), dec_ref[0, h], F32))
    lg_b = -jnp.exp(jnp.full((c, c), dec_ref[1, h], F32))
    row = lax.broadcasted_iota(jnp.int32, (c, c), 0).astype(F32)
    col = lax.broadcasted_iota(jnp.int32, (c, c), 1).astype(F32)
    rel = row - col
    decay = jnp.where(rel >= 0, jnp.exp(jnp.maximum(rel, 0.0) * lg_f),
                      jnp.exp(jnp.maximum(-rel, 0.0) * lg_b))
    kdec_f = jnp.exp((c - 1 - row) * lg_f)
    kdec_b = jnp.exp(row * lg_b)
    qdec_f = jnp.exp((row + 1.0) * lg_f)
    qdec_b = jnp.exp((c - row) * lg_b)
    g_f = jnp.exp(c * lg_f)
    g_b = jnp.exp(c * lg_b)

    def rows(i):
        return pl.ds(pl.multiple_of(i * c, c), c)

    def kv_body(i, carry):
        k = k_ref[rows(i), :].astype(F32)
        v = v_ref[rows(i), :]
        kvf_ref[i] = _dot((k * kdec_f).T.astype(BF16), v)
        kvb_ref[i] = _dot((k * kdec_b).T.astype(BF16), v)
        return carry

    lax.fori_loop(0, n_chunks, kv_body, 0)

    def fwd_state(i, state):
        st_ref[i, :c, :] = state.astype(BF16)
        return g_f * state + kvf_ref[i]

    lax.fori_loop(0, n_chunks, fwd_state, jnp.zeros((c, c), F32))

    def bwd_state(j, state):
        i = n_chunks - 1 - j
        st_ref[i, c:, :] = state.astype(BF16)
        return g_b * state + kvb_ref[i]

    lax.fori_loop(0, n_chunks, bwd_state, jnp.zeros((c, c), F32))

    gain = gain_ref[...]

    def out_body(i, carry):
        q = q_ref[rows(i), :]
        scores = _dot_nt(q, k_ref[rows(i), :])
        inner = _dot((scores * decay).astype(BF16), v_ref[rows(i), :])
        q32 = q.astype(F32)
        qd = jnp.concatenate([(q32 * qdec_f).astype(BF16), (q32 * qdec_b).astype(BF16)], axis=1)
        y = inner + _dot(qd, st_ref[i])
        mu = jnp.mean(y, axis=-1, keepdims=True)
        yc = y - mu
        var = jnp.mean(yc * yc, axis=-1, keepdims=True)
        yn = yc * lax.rsqrt(var + EPS) * gain
        z_ref[rows(i), :] = (yn * s_ref[rows(i), :].astype(F32)).astype(BF16)
        return carry

    lax.fori_loop(0, n_chunks, out_body, 0)


def _ret_call(dec, qr, kr, vr, sr, gain, batch, seq):
    t, w = qr.shape
    c = RET_CHUNK
    n_chunks = seq // c
    blk = pl.BlockSpec((seq, RET_DK), lambda b, h: (b, h))
    return pl.pallas_call(
        _ret_kernel,
        grid=(batch, RET_HEADS),
        in_specs=[pl.BlockSpec(memory_space=pltpu.SMEM), blk, blk, blk, blk,
                  pl.BlockSpec((1, RET_DK), lambda b, h: (0, h))],
        out_specs=blk,
        out_shape=jax.ShapeDtypeStruct((t, w), BF16),
        scratch_shapes=[pltpu.VMEM((n_chunks, c, c), F32), pltpu.VMEM((n_chunks, c, c), F32),
                        pltpu.VMEM((n_chunks, 2 * c, c), BF16)],
        compiler_params=_cparams(2),
        name="retention",
    )(dec, qr, kr, vr, sr, gain)


def _att_kernel(sink_ref, q_ref, k_ref, v_ref, o_ref):
    blk = ATT_BLOCK
    n_blocks = q_ref.shape[0] // blk
    grp = ATT_Q_HEADS // ATT_KV_HEADS
    qpos = lax.broadcasted_iota(jnp.int32, (2 * blk, blk), 0) % blk
    kpos = lax.broadcasted_iota(jnp.int32, (2 * blk, blk), 1)
    prev_ok = kpos >= qpos
    next_ok = kpos <= qpos
    top = lax.broadcasted_iota(jnp.int32, (2 * blk, 1), 0) < blk
    neg_inf = jnp.float32(-jnp.inf)

    def body(i, carry):
        r_cur = pl.ds(pl.multiple_of(i * blk, blk), blk)
        r_prev = pl.ds(pl.multiple_of(jnp.maximum(i - 1, 0) * blk, blk), blk)
        r_next = pl.ds(pl.multiple_of(jnp.minimum(i + 1, n_blocks - 1) * blk, blk), blk)
        has_prev = i > 0
        has_next = i < n_blocks - 1
        for g in range(ATT_KV_HEADS):
            base = 2 * LANES * g
            q = jnp.concatenate([q_ref[r_cur, base:base + LANES],
                                 q_ref[r_cur, base + LANES:base + 2 * LANES]], axis=0)
            acc = jnp.zeros((2 * blk, LANES), F32)
            for p in range(2):
                kv = slice((2 * g + p) * LANES, (2 * g + p + 1) * LANES)
                snk = jnp.where(top, sink_ref[grp * g + p], sink_ref[grp * g + 2 + p])
                s_prev = jnp.where(jnp.logical_and(prev_ok, has_prev), _dot_nt(q, k_ref[r_prev, kv]), neg_inf)
                s_cur = _dot_nt(q, k_ref[r_cur, kv])
                s_next = jnp.where(jnp.logical_and(next_ok, has_next), _dot_nt(q, k_ref[r_next, kv]), neg_inf)
                m = jnp.maximum(jnp.maximum(jnp.max(s_prev, axis=-1, keepdims=True),
                                            jnp.max(s_cur, axis=-1, keepdims=True)),
                                jnp.maximum(jnp.max(s_next, axis=-1, keepdims=True), snk))
                e_prev = jnp.exp(s_prev - m)
                e_cur = jnp.exp(s_cur - m)
                e_next = jnp.exp(s_next - m)
                denom = (jnp.sum(e_prev, axis=-1, keepdims=True) + jnp.sum(e_cur, axis=-1, keepdims=True)
                         + jnp.sum(e_next, axis=-1, keepdims=True) + jnp.exp(snk - m))
                inv = 1.0 / denom
                acc = acc + _dot((e_prev * inv).astype(BF16), v_ref[r_prev, kv])
                acc = acc + _dot((e_cur * inv).astype(BF16), v_ref[r_cur, kv])
                acc = acc + _dot((e_next * inv).astype(BF16), v_ref[r_next, kv])
            o_ref[r_cur, base:base + LANES] = acc[:blk].astype(BF16)
            o_ref[r_cur, base + LANES:base + 2 * LANES] = acc[blk:].astype(BF16)
        return carry

    lax.fori_loop(0, n_blocks, body, 0)


def _att_call(sink, qa, k4, v4, batch, seq):
    t, w = qa.shape
    blk = pl.BlockSpec((seq, w), lambda b: (b, 0))
    return pl.pallas_call(
        _att_kernel,
        grid=(batch,),
        in_specs=[pl.BlockSpec(memory_space=pltpu.SMEM), blk, blk, blk],
        out_specs=blk,
        out_shape=jax.ShapeDtypeStruct((t, w), BF16),
        compiler_params=_cparams(1),
        name="attention",
    )(sink, qa, k4, v4)


def _tail_kernel(x_ref, zr_ref, oa_ref, sgr_ref, sga_ref, wro_ref, wao_ref, wout_ref,
                 gn_ref, wgu_ref, wo_ref, gfin_ref, y_ref, h_ref, a_ref):
    y_ret = _dot(zr_ref[...], wro_ref[...])
    y_att = _dot(oa_ref[...], wao_ref[...])
    merged = sgr_ref[...].astype(F32) * y_ret + sga_ref[...].astype(F32) * y_att
    x2 = x_ref[...] + _dot(merged.astype(BF16), wout_ref[...])
    x3 = _swiglu_residual(x2, gn_ref, wgu_ref, wo_ref, h_ref, a_ref)
    y_ref[...] = _rms(x3, gfin_ref[...])


def _tail_call(x1, zr, oa, sgr, sga, wro, wao, wout, gn, wgu, wo, gfin):
    t, d = x1.shape
    tm = TOKEN_TILE
    ffp = wo.shape[0]
    tok = lambda width: pl.BlockSpec((tm, width), lambda i: (i, 0))
    return pl.pallas_call(
        _tail_kernel,
        grid=(t // tm,),
        in_specs=[tok(d), tok(zr.shape[1]), tok(oa.shape[1]), tok(d), tok(d),
                  _resident(wro.shape), _resident(wao.shape), _resident(wout.shape),
                  _resident(gn.shape), _resident(wgu.shape), _resident(wo.shape),
                  _resident(gfin.shape)],
        out_specs=tok(d),
        out_shape=jax.ShapeDtypeStruct((t, d), F32),
        scratch_shapes=[pltpu.VMEM((tm, d), BF16), pltpu.VMEM((tm, ffp), BF16)],
        compiler_params=_cparams(1),
        name="tail",
    )(x1, zr, oa, sgr, sga, wro, wao, wout, gn, wgu, wo, gfin)


def _prep_ffn(w_in, w_out):
    d, two_ff = w_in.shape
    ff = two_ff // 2
    ffp = -(-ff // FF_CHUNK) * FF_CHUNK
    n = ffp // FF_CHUNK
    pad = ((0, 0), (0, ffp - ff))
    g = jnp.pad(w_in[:, :ff], pad).reshape(d, n, FF_CHUNK)
    u = jnp.pad(w_in[:, ff:], pad).reshape(d, n, FF_CHUNK)
    wgu = jnp.concatenate([g, u], axis=2).transpose(1, 0, 2).astype(BF16)
    wo = jnp.pad(w_out, ((0, ffp - ff), (0, 0))).astype(BF16)
    return wgu, wo


def _prep_w_in(w):
    d = w.shape[0]
    ret_w = RET_HEADS * RET_DK
    qw = ATT_Q_HEADS * ATT_HD
    kvw = ATT_KV_HEADS * ATT_HD
    a0 = 4 * ret_w
    half = ATT_HD // 2
    wq = w[:, a0:a0 + qw].reshape(d, ATT_Q_HEADS // 2, 2, 2, half).transpose(0, 1, 3, 2, 4).reshape(d, qw)
    wk = w[:, a0 + qw:a0 + qw + kvw].reshape(d, ATT_KV_HEADS // 2, 2, 2, half).transpose(0, 1, 3, 2, 4).reshape(d, kvw)
    return jnp.concatenate([w[:, :a0], wq, wk, w[:, a0 + qw + kvw:]], axis=1).astype(BF16)


def _rope_tables(seq):
    pos = jnp.arange(seq, dtype=F32)[:, None]

    def cs(half):
        inv_freq = ROPE_THETA ** (-jnp.arange(half, dtype=F32) / half)
        ang = pos * inv_freq[None, :]
        return jnp.cos(ang), jnp.sin(ang)

    c, s = cs(RET_DK // 2)
    cosr = jnp.concatenate([c, c], axis=1)
    sinr = jnp.concatenate([-s, s], axis=1)
    c, s = cs(ATT_HD // 2)
    cosa = jnp.concatenate([c, c, c, c], axis=1)
    sina = jnp.concatenate([-s, -s, s, s], axis=1)
    return cosr, sinr, cosa, sina


def kernel(x, norm_ffn1, ffn1_w_in, ffn1_w_out, norm_mix, w_in, ret_decay_fwd, ret_decay_bwd,
           ret_gn_gain, w_ret_out, att_sink, w_att_out, w_out, norm_ffn2, ffn2_w_in, ffn2_w_out,
           norm_final):
    b, s, d = x.shape
    assert ffn1_w_in.shape[0] == 1, "single-layer block"
    assert s % TOKEN_TILE == 0 and s % RET_CHUNK == 0 and s // ATT_BLOCK >= 2
    xt = x.reshape(b * s, d)
    wgu1, wo1 = _prep_ffn(ffn1_w_in[0], ffn1_w_out[0])
    wgu2, wo2 = _prep_ffn(ffn2_w_in[0], ffn2_w_out[0])
    cosr, sinr, cosa, sina = _rope_tables(s)

    x1 = _ffn_call(xt, norm_ffn1, wgu1, wo1)
    qr, kr, vr, sr, qa, k4, v4, sgr, sga = _proj_call(
        x1, norm_mix, _prep_w_in(w_in[0]), cosr, sinr, cosa, sina, s)
    dec = jnp.concatenate([ret_decay_fwd, ret_decay_bwd], axis=0)
    zr = _ret_call(dec, qr, kr, vr, sr, ret_gn_gain, b, s)
    oa = _att_call(att_sink[0], qa, k4, v4, b, s)
    y = _tail_call(x1, zr, oa, sgr, sga, w_ret_out[0].astype(BF16), w_att_out[0].astype(BF16),
                   w_out[0].astype(BF16), norm_ffn2, wgu2, wo2, norm_final.reshape(1, d))
    return y.reshape(b, s, d)
```

```python
import functools

import jax
import jax.numpy as jnp
import numpy as np
from jax import lax
from jax.experimental import pallas as pl
from jax.experimental.pallas import tpu as pltpu

F32 = jnp.float32
BF16 = jnp.bfloat16

EPS = 1e-6
ROPE_THETA = 10000.0
RET_HEADS = 4
RET_DK = 128
RET_TILE = 256
ATT_Q_HEADS = 8
ATT_KV_HEADS = 2
ATT_HD = 64
ATT_BLOCK = 128
LANES = 128
FF_CHUNK = 256
TOKEN_TILE = 512
VMEM_LIMIT = 56 * 1024 * 1024


def _cparams(n_grid):
    return pltpu.CompilerParams(dimension_semantics=("arbitrary",) * n_grid,
                                vmem_limit_bytes=VMEM_LIMIT)


def _resident(shape):
    zeros = (0,) * len(shape)
    return pl.BlockSpec(shape, lambda *_: zeros, pipeline_mode=pl.Buffered(1))


def _rms(x, g):
    y = x * lax.rsqrt(jnp.mean(x * x, axis=-1, keepdims=True) + EPS)
    return y * g


def _dot(a, b):
    return jnp.dot(a, b, preferred_element_type=F32)


def _dot_nt(a, b):
    return lax.dot_general(a, b, (((1,), (1,)), ((), ())), preferred_element_type=F32)


def _swiglu_residual(x, gn_ref, wgu_ref, wo_ref, h_ref, a_ref):
    h_ref[...] = _rms(x, gn_ref[...]).astype(BF16)
    n_chunks = wgu_ref.shape[0]
    fc = wgu_ref.shape[2] // 2
    for c in range(n_chunks):
        gu = _dot(h_ref[...], wgu_ref[c])
        g = gu[:, :fc]
        u = gu[:, fc:]
        a_ref[:, c * fc:(c + 1) * fc] = (jax.nn.silu(g) * u).astype(BF16)
    return x + 0.5 * _dot(a_ref[...], wo_ref[...])


def _ffn_kernel(x_ref, gn_ref, wgu_ref, wo_ref, o_ref, h_ref, a_ref):
    o_ref[...] = _swiglu_residual(x_ref[...], gn_ref, wgu_ref, wo_ref, h_ref, a_ref)


def _ffn_call(x, gn, wgu, wo):
    t, d = x.shape
    tm = TOKEN_TILE
    ffp = wo.shape[0]
    return pl.pallas_call(
        _ffn_kernel,
        grid=(t // tm,),
        in_specs=[pl.BlockSpec((tm, d), lambda i: (i, 0)),
                  _resident(gn.shape), _resident(wgu.shape), _resident(wo.shape)],
        out_specs=pl.BlockSpec((tm, d), lambda i: (i, 0)),
        out_shape=jax.ShapeDtypeStruct((t, d), F32),
        scratch_shapes=[pltpu.VMEM((tm, d), BF16), pltpu.VMEM((tm, ffp), BF16)],
        compiler_params=_cparams(1),
        name="ffn1",
    )(x, gn, wgu, wo)


def _rope128(y, cos, sin_signed):
    return y * cos + pltpu.roll(y, 64, 1) * sin_signed


def _proj_kernel(x_ref, gn_ref, w_ref, cosr_ref, sinr_ref, cosa_ref, sina_ref,
                 qr_ref, kr_ref, vr_ref, sr_ref, qa_ref, k4_ref, v4_ref, sgr_ref, sga_ref,
                 h_ref, *, ret_w, att_qw):
    h_ref[...] = _rms(x_ref[...], gn_ref[...]).astype(BF16)

    def proj(a, b):
        return _dot(h_ref[...], w_ref[:, a:b])

    cosr, sinr = cosr_ref[...], sinr_ref[...]
    cosa, sina = cosa_ref[...], sina_ref[...]
    c0 = 0
    y = proj(c0, c0 + ret_w)
    for j in range(ret_w // LANES):
        sl = slice(j * LANES, (j + 1) * LANES)
        qr_ref[:, sl] = _rope128(y[:, sl], cosr, sinr).astype(BF16)
    c0 += ret_w
    y = proj(c0, c0 + ret_w)
    for j in range(ret_w // LANES):
        sl = slice(j * LANES, (j + 1) * LANES)
        kr_ref[sl, :] = (_rope128(y[:, sl], cosr, sinr) * (RET_DK ** -0.5)).T.astype(BF16)
    c0 += ret_w
    vr_ref[...] = proj(c0, c0 + ret_w).astype(BF16)
    c0 += ret_w
    sr_ref[...] = jax.nn.silu(proj(c0, c0 + ret_w)).astype(BF16)
    c0 += ret_w
    y = proj(c0, c0 + att_qw)
    for j in range(att_qw // LANES):
        sl = slice(j * LANES, (j + 1) * LANES)
        qa_ref[:, sl] = (_rope128(y[:, sl], cosa, sina) * (ATT_HD ** -0.5)).astype(BF16)
    c0 += att_qw
    y = proj(c0, c0 + 2 * LANES)
    kk = _rope128(y[:, :LANES], cosa, sina)
    vv = y[:, LANES:]
    lane = lax.broadcasted_iota(jnp.int32, kk.shape, 1)
    even = (lane // 32) % 2 == 0
    low = lane < 64
    k4_ref[:, 0 * LANES:1 * LANES] = jnp.where(even, kk, 0.0).astype(BF16)
    k4_ref[:, 1 * LANES:2 * LANES] = jnp.where(even, 0.0, pltpu.roll(kk, 32, 1)).astype(BF16)
    k4_ref[:, 2 * LANES:3 * LANES] = jnp.where(even, pltpu.roll(kk, 96, 1), 0.0).astype(BF16)
    k4_ref[:, 3 * LANES:4 * LANES] = jnp.where(even, 0.0, kk).astype(BF16)
    vroll = pltpu.roll(vv, 64, 1)
    v4_ref[:, 0 * LANES:1 * LANES] = jnp.where(low, vv, 0.0).astype(BF16)
    v4_ref[:, 1 * LANES:2 * LANES] = jnp.where(low, 0.0, vroll).astype(BF16)
    v4_ref[:, 2 * LANES:3 * LANES] = jnp.where(low, vroll, 0.0).astype(BF16)
    v4_ref[:, 3 * LANES:4 * LANES] = jnp.where(low, 0.0, vv).astype(BF16)
    c0 += 2 * LANES
    d = x_ref.shape[1]
    sgr_ref[...] = jax.nn.sigmoid(proj(c0, c0 + d)).astype(BF16)
    c0 += d
    sga_ref[...] = jax.nn.sigmoid(proj(c0, c0 + d)).astype(BF16)


def _proj_call(x1, gn, w, cosr, sinr, cosa, sina, seq):
    t, d = x1.shape
    tm = TOKEN_TILE
    ret_w = RET_HEADS * RET_DK
    att_qw = ATT_Q_HEADS * ATT_HD
    n_pos = seq // tm
    tok = lambda width: pl.BlockSpec((tm, width), lambda i: (i, 0))
    tab = pl.BlockSpec((tm, LANES), lambda i: (i % n_pos, 0))
    widths = (ret_w, ret_w, ret_w, ret_w, att_qw, 4 * LANES, 4 * LANES, d, d)
    out_specs = [tok(wd) for wd in widths]
    out_shape = [jax.ShapeDtypeStruct((t, wd), BF16) for wd in widths]
    out_specs[1] = pl.BlockSpec((ret_w, tm), lambda i: (0, i))
    out_shape[1] = jax.ShapeDtypeStruct((ret_w, t), BF16)
    return pl.pallas_call(
        functools.partial(_proj_kernel, ret_w=ret_w, att_qw=att_qw),
        grid=(t // tm,),
        in_specs=[tok(d), _resident(gn.shape), _resident(w.shape), tab, tab, tab, tab],
        out_specs=out_specs,
        out_shape=out_shape,
        scratch_shapes=[pltpu.VMEM((tm, d), BF16)],
        compiler_params=_cparams(1),
        name="proj",
    )(x1, gn, w, cosr, sinr, cosa, sina)


def _ret_kernel(dec_ref, q_ref, kt_ref, v_ref, s_ref, gain_ref, z_ref, kv_ref, st_ref):
    c = RET_TILE
    dk = kt_ref.shape[0]
    n_chunks = q_ref.shape[0] // c
    h = pl.program_id(1)

    def log_gamma(direction, shape):
        return -jnp.exp(jnp.full(shape, dec_ref[direction, h], F32))

    def iota(shape, axis):
        return lax.broadcasted_iota(jnp.int32, shape, axis).astype(F32)

    rel = iota((c, c), 0) - iota((c, c), 1)
    decay = jnp.where(rel >= 0, jnp.exp(jnp.maximum(rel, 0.0) * log_gamma(0, (c, c))),
                      jnp.exp(jnp.maximum(-rel, 0.0) * log_gamma(1, (c, c))))
    key = iota((dk, c), 1)
    kdec_f = jnp.exp((c - 1 - key) * log_gamma(0, (dk, c)))
    kdec_b = jnp.exp(key * log_gamma(1, (dk, c)))
    qry = iota((c, dk), 0)
    qdec_f = jnp.exp((qry + 1.0) * log_gamma(0, (c, dk)))
    qdec_b = jnp.exp((c - qry) * log_gamma(1, (c, dk)))
    g_f = jnp.exp(c * log_gamma(0, (dk, dk)))
    g_b = jnp.exp(c * log_gamma(1, (dk, dk)))

    def span(i):
        return slice(i * c, (i + 1) * c)

    for i in range(n_chunks):
        kt = kt_ref[:, span(i)].astype(F32)
        lhs = jnp.concatenate([(kt * kdec_f).astype(BF16), (kt * kdec_b).astype(BF16)], axis=0)
        kv_ref[i] = _dot(lhs, v_ref[span(i), :])

    state = jnp.zeros((dk, dk), F32)
    for i in range(n_chunks):
        st_ref[i, :dk, :] = state.astype(BF16)
        state = g_f * state + kv_ref[i, :dk, :]
    state = jnp.zeros((dk, dk), F32)
    for i in reversed(range(n_chunks)):
        st_ref[i, dk:, :] = state.astype(BF16)
        state = g_b * state + kv_ref[i, dk:, :]

    gain = gain_ref[...]
    for i in range(n_chunks):
        q = q_ref[span(i), :]
        scores = _dot(q, kt_ref[:, span(i)])
        inner = _dot((scores * decay).astype(BF16), v_ref[span(i), :])
        q32 = q.astype(F32)
        qd = jnp.concatenate([(q32 * qdec_f).astype(BF16), (q32 * qdec_b).astype(BF16)], axis=1)
        y = inner + _dot(qd, st_ref[i])
        mu = jnp.mean(y, axis=-1, keepdims=True)
        yc = y - mu
        var = jnp.mean(yc * yc, axis=-1, keepdims=True)
        yn = yc * lax.rsqrt(var + EPS) * gain
        z_ref[span(i), :] = (yn * s_ref[span(i), :].astype(F32)).astype(BF16)


def _ret_call(dec, qr, krt, vr, sr, gain, batch, seq):
    t, w = qr.shape
    c = RET_TILE
    n_chunks = seq // c
    blk = pl.BlockSpec((seq, RET_DK), lambda b, h: (b, h))
    return pl.pallas_call(
        _ret_kernel,
        grid=(batch, RET_HEADS),
        in_specs=[pl.BlockSpec(memory_space=pltpu.SMEM), blk,
                  pl.BlockSpec((RET_DK, seq), lambda b, h: (h, b)), blk, blk,
                  pl.BlockSpec((1, RET_DK), lambda b, h: (0, h))],
        out_specs=blk,
        out_shape=jax.ShapeDtypeStruct((t, w), BF16),
        scratch_shapes=[pltpu.VMEM((n_chunks, 2 * RET_DK, RET_DK), F32),
                        pltpu.VMEM((n_chunks, 2 * RET_DK, RET_DK), BF16)],
        compiler_params=_cparams(2),
        name="retention",
    )(dec, qr, krt, vr, sr, gain)


def _att_kernel(sink_ref, q_ref, k_ref, v_ref, o_ref):
    blk = ATT_BLOCK
    n_blocks = q_ref.shape[0] // blk
    grp = ATT_Q_HEADS // ATT_KV_HEADS
    qpos = lax.broadcasted_iota(jnp.int32, (2 * blk, blk), 0) % blk
    kpos = lax.broadcasted_iota(jnp.int32, (2 * blk, blk), 1)
    prev_ok = kpos >= qpos
    next_ok = kpos <= qpos
    top = lax.broadcasted_iota(jnp.int32, (2 * blk, 1), 0) < blk
    neg_inf = jnp.float32(-jnp.inf)

    def body(i, carry):
        r_cur = pl.ds(pl.multiple_of(i * blk, blk), blk)
        r_prev = pl.ds(pl.multiple_of(jnp.maximum(i - 1, 0) * blk, blk), blk)
        r_next = pl.ds(pl.multiple_of(jnp.minimum(i + 1, n_blocks - 1) * blk, blk), blk)
        has_prev = i > 0
        has_next = i < n_blocks - 1
        for g in range(ATT_KV_HEADS):
            base = 2 * LANES * g
            q = jnp.concatenate([q_ref[r_cur, base:base + LANES],
                                 q_ref[r_cur, base + LANES:base + 2 * LANES]], axis=0)
            acc = jnp.zeros((2 * blk, LANES), F32)
            for p in range(2):
                kv = slice((2 * g + p) * LANES, (2 * g + p + 1) * LANES)
                snk = jnp.where(top, sink_ref[grp * g + p], sink_ref[grp * g + 2 + p])
                s_prev = jnp.where(jnp.logical_and(prev_ok, has_prev), _dot_nt(q, k_ref[r_prev, kv]), neg_inf)
                s_cur = _dot_nt(q, k_ref[r_cur, kv])
                s_next = jnp.where(jnp.logical_and(next_ok, has_next), _dot_nt(q, k_ref[r_next, kv]), neg_inf)
                m = jnp.maximum(jnp.maximum(jnp.max(s_prev, axis=-1, keepdims=True),
                                            jnp.max(s_cur, axis=-1, keepdims=True)),
                                jnp.maximum(jnp.max(s_next, axis=-1, keepdims=True), snk))
                e_prev = jnp.exp(s_prev - m)
                e_cur = jnp.exp(s_cur - m)
                e_next = jnp.exp(s_next - m)
                denom = (jnp.sum(e_prev, axis=-1, keepdims=True) + jnp.sum(e_cur, axis=-1, keepdims=True)
                         + jnp.sum(e_next, axis=-1, keepdims=True) + jnp.exp(snk - m))
                inv = 1.0 / denom
                acc = acc + _dot((e_prev * inv).astype(BF16), v_ref[r_prev, kv])
                acc = acc + _dot((e_cur * inv).astype(BF16), v_ref[r_cur, kv])
                acc = acc + _dot((e_next * inv).astype(BF16), v_ref[r_next, kv])
            o_ref[r_cur, base:base + LANES] = acc[:blk].astype(BF16)
            o_ref[r_cur, base + LANES:base + 2 * LANES] = acc[blk:].astype(BF16)
        return carry

    lax.fori_loop(0, n_blocks, body, 0)


def _att_call(sink, qa, k4, v4, batch, seq):
    t, w = qa.shape
    blk = pl.BlockSpec((seq, w), lambda b: (b, 0))
    return pl.pallas_call(
        _att_kernel,
        grid=(batch,),
        in_specs=[pl.BlockSpec(memory_space=pltpu.SMEM), blk, blk, blk],
        out_specs=blk,
        out_shape=jax.ShapeDtypeStruct((t, w), BF16),
        compiler_params=_cparams(1),
        name="attention",
    )(sink, qa, k4, v4)


def _tail_kernel(x_ref, zr_ref, oa_ref, sgr_ref, sga_ref, wro_ref, wao_ref, wout_ref,
                 gn_ref, wgu_ref, wo_ref, gfin_ref, y_ref, h_ref, a_ref):
    y_ret = _dot(zr_ref[...], wro_ref[...])
    y_att = _dot(oa_ref[...], wao_ref[...])
    merged = sgr_ref[...].astype(F32) * y_ret + sga_ref[...].astype(F32) * y_att
    x2 = x_ref[...] + _dot(merged.astype(BF16), wout_ref[...])
    x3 = _swiglu_residual(x2, gn_ref, wgu_ref, wo_ref, h_ref, a_ref)
    y_ref[...] = _rms(x3, gfin_ref[...])


def _tail_call(x1, zr, oa, sgr, sga, wro, wao, wout, gn, wgu, wo, gfin):
    t, d = x1.shape
    tm = TOKEN_TILE
    ffp = wo.shape[0]
    tok = lambda width: pl.BlockSpec((tm, width), lambda i: (i, 0))
    return pl.pallas_call(
        _tail_kernel,
        grid=(t // tm,),
        in_specs=[tok(d), tok(zr.shape[1]), tok(oa.shape[1]), tok(d), tok(d),
                  _resident(wro.shape), _resident(wao.shape), _resident(wout.shape),
                  _resident(gn.shape), _resident(wgu.shape), _resident(wo.shape),
                  _resident(gfin.shape)],
        out_specs=tok(d),
        out_shape=jax.ShapeDtypeStruct((t, d), F32),
        scratch_shapes=[pltpu.VMEM((tm, d), BF16), pltpu.VMEM((tm, ffp), BF16)],
        compiler_params=_cparams(1),
        name="tail",
    )(x1, zr, oa, sgr, sga, wro, wao, wout, gn, wgu, wo, gfin)


def _prep_ffn(w_in, w_out):
    d, two_ff = w_in.shape
    ff = two_ff // 2
    ffp = -(-ff // FF_CHUNK) * FF_CHUNK
    n = ffp // FF_CHUNK
    pad = ((0, 0), (0, ffp - ff))
    g = jnp.pad(w_in[:, :ff], pad).reshape(d, n, FF_CHUNK)
    u = jnp.pad(w_in[:, ff:], pad).reshape(d, n, FF_CHUNK)
    wgu = jnp.concatenate([g, u], axis=2).transpose(1, 0, 2).astype(BF16)
    wo = jnp.pad(w_out, ((0, ffp - ff), (0, 0))).astype(BF16)
    return wgu, wo


def _prep_w_in(w):
    d = w.shape[0]
    ret_w = RET_HEADS * RET_DK
    qw = ATT_Q_HEADS * ATT_HD
    kvw = ATT_KV_HEADS * ATT_HD
    a0 = 4 * ret_w
    half = ATT_HD // 2
    wq = w[:, a0:a0 + qw].reshape(d, ATT_Q_HEADS // 2, 2, 2, half).transpose(0, 1, 3, 2, 4).reshape(d, qw)
    wk = w[:, a0 + qw:a0 + qw + kvw].reshape(d, ATT_KV_HEADS // 2, 2, 2, half).transpose(0, 1, 3, 2, 4).reshape(d, kvw)
    return jnp.concatenate([w[:, :a0], wq, wk, w[:, a0 + qw + kvw:]], axis=1).astype(BF16)


def _rope_tables(seq):
    pos = jnp.arange(seq, dtype=F32)[:, None]

    def cs(half):
        inv_freq = ROPE_THETA ** (-jnp.arange(half, dtype=F32) / half)
        ang = pos * inv_freq[None, :]
        return jnp.cos(ang), jnp.sin(ang)

    c, s = cs(RET_DK // 2)
    cosr = jnp.concatenate([c, c], axis=1)
    sinr = jnp.concatenate([-s, s], axis=1)
    c, s = cs(ATT_HD // 2)
    cosa = jnp.concatenate([c, c, c, c], axis=1)
    sina = jnp.concatenate([-s, -s, s, s], axis=1)
    return cosr, sinr, cosa, sina


def kernel(x, norm_ffn1, ffn1_w_in, ffn1_w_out, norm_mix, w_in, ret_decay_fwd, ret_decay_bwd,
           ret_gn_gain, w_ret_out, att_sink, w_att_out, w_out, norm_ffn2, ffn2_w_in, ffn2_w_out,
           norm_final):
    b, s, d = x.shape
    assert ffn1_w_in.shape[0] == 1, "single-layer block"
    assert s % TOKEN_TILE == 0 and s % RET_TILE == 0 and s // ATT_BLOCK >= 2
    xt = x.reshape(b * s, d)
    wgu1, wo1 = _prep_ffn(ffn1_w_in[0], ffn1_w_out[0])
    wgu2, wo2 = _prep_ffn(ffn2_w_in[0], ffn2_w_out[0])
    cosr, sinr, cosa, sina = _rope_tables(s)

    x1 = _ffn_call(xt, norm_ffn1, wgu1, wo1)
    qr, kr, vr, sr, qa, k4, v4, sgr, sga = _proj_call(
        x1, norm_mix, _prep_w_in(w_in[0]), cosr, sinr, cosa, sina, s)
    dec = jnp.concatenate([ret_decay_fwd, ret_decay_bwd], axis=0)
    zr = _ret_call(dec, qr, kr, vr, sr, ret_gn_gain, b, s)
    oa = _att_call(att_sink[0], qa, k4, v4, b, s)
    y = _tail_call(x1, zr, oa, sgr, sga, w_ret_out[0].astype(BF16), w_att_out[0].astype(BF16),
                   w_out[0].astype(BF16), norm_ffn2, wgu2, wo2, norm_final.reshape(1, d))
    return y.reshape(b, s, d)
```

```python
import functools

import jax
import jax.numpy as jnp
import numpy as np
from jax import lax
from jax.experimental import pallas as pl
from jax.experimental.pallas import tpu as pltpu

F32 = jnp.float32
BF16 = jnp.bfloat16

EPS = 1e-6
ROPE_THETA = 10000.0
RET_HEADS = 4
RET_DK = 128
RET_TILE = 256
ATT_Q_HEADS = 8
ATT_KV_HEADS = 2
ATT_HD = 64
ATT_BLOCK = 128
ATT_TILE = 1024
ATT_SCORES_AHEAD = 3
LANES = 128
FF_CHUNK = 256
TOKEN_TILE = 512
VMEM_LIMIT = 56 * 1024 * 1024


def _cparams(n_grid):
    return pltpu.CompilerParams(dimension_semantics=("arbitrary",) * n_grid,
                                vmem_limit_bytes=VMEM_LIMIT)


def _resident(shape):
    zeros = (0,) * len(shape)
    return pl.BlockSpec(shape, lambda *_: zeros, pipeline_mode=pl.Buffered(1))


def _rms(x, g):
    y = x * lax.rsqrt(jnp.mean(x * x, axis=-1, keepdims=True) + EPS)
    return y * g


def _dot(a, b):
    return jnp.dot(a, b, preferred_element_type=F32)


def _dot_nt(a, b):
    return lax.dot_general(a, b, (((1,), (1,)), ((), ())), preferred_element_type=F32)


def _swiglu_residual(x, gn_ref, wgu_ref, wo_ref, h_ref, a_ref):
    h_ref[...] = _rms(x, gn_ref[...]).astype(BF16)
    fc = FF_CHUNK
    n_chunks = wgu_ref.shape[1] // (2 * fc)
    for c in range(n_chunks):
        gu = _dot(h_ref[...], wgu_ref[:, 2 * fc * c:2 * fc * (c + 1)])
        g = gu[:, :fc]
        u = gu[:, fc:]
        a_ref[:, c * fc:(c + 1) * fc] = (jax.nn.silu(g) * u).astype(BF16)
    return x + 0.5 * _dot(a_ref[...], wo_ref[...])


def _ffn_kernel(x_ref, gn_ref, wgu_ref, wo_ref, o_ref, h_ref, a_ref):
    o_ref[...] = _swiglu_residual(x_ref[...], gn_ref, wgu_ref, wo_ref, h_ref, a_ref)


def _ffn_call(x, gn, wgu, wo):
    t, d = x.shape
    tm = TOKEN_TILE
    ffp = wo.shape[0]
    return pl.pallas_call(
        _ffn_kernel,
        grid=(t // tm,),
        in_specs=[pl.BlockSpec((tm, d), lambda i: (i, 0)),
                  _resident(gn.shape), _resident(wgu.shape), _resident(wo.shape)],
        out_specs=pl.BlockSpec((tm, d), lambda i: (i, 0)),
        out_shape=jax.ShapeDtypeStruct((t, d), F32),
        scratch_shapes=[pltpu.VMEM((tm, d), BF16), pltpu.VMEM((tm, ffp), BF16)],
        compiler_params=_cparams(1),
        name="ffn1",
    )(x, gn, wgu, wo)


def _rope128(y, cos, sin_signed):
    return y * cos + pltpu.roll(y, 64, 1) * sin_signed


def _proj_kernel(x_ref, gn_ref, w_ref, cosr_ref, sinr_ref, cosa_ref, sina_ref,
                 qr_ref, kr_ref, vr_ref, sr_ref, qa_ref, k4_ref, va_ref, sgr_ref, sga_ref,
                 h_ref, *, ret_w, att_qw):
    h_ref[...] = _rms(x_ref[...], gn_ref[...]).astype(BF16)

    def proj(a, b):
        return _dot(h_ref[...], w_ref[:, a:b])

    cosr, sinr = cosr_ref[...], sinr_ref[...]
    cosa, sina = cosa_ref[...], sina_ref[...]
    c0 = 0
    y = proj(c0, c0 + ret_w)
    for j in range(ret_w // LANES):
        sl = slice(j * LANES, (j + 1) * LANES)
        qr_ref[:, sl] = _rope128(y[:, sl], cosr, sinr).astype(BF16)
    c0 += ret_w
    y = proj(c0, c0 + ret_w)
    for j in range(ret_w // LANES):
        sl = slice(j * LANES, (j + 1) * LANES)
        kr_ref[sl, :] = (_rope128(y[:, sl], cosr, sinr) * (RET_DK ** -0.5)).T.astype(BF16)
    c0 += ret_w
    vr_ref[...] = proj(c0, c0 + ret_w).astype(BF16)
    c0 += ret_w
    sr_ref[...] = jax.nn.silu(proj(c0, c0 + ret_w)).astype(BF16)
    c0 += ret_w
    y = proj(c0, c0 + att_qw)
    for j in range(att_qw // LANES):
        sl = slice(j * LANES, (j + 1) * LANES)
        qa_ref[sl, :] = (_rope128(y[:, sl], cosa, sina) * (ATT_HD ** -0.5)).T.astype(BF16)
    c0 += att_qw
    y = proj(c0, c0 + 2 * LANES)
    kk = _rope128(y[:, :LANES], cosa, sina)
    even = (lax.broadcasted_iota(jnp.int32, kk.shape, 1) // 32) % 2 == 0
    k4_ref[:, 0 * LANES:1 * LANES] = jnp.where(even, kk, 0.0).astype(BF16)
    k4_ref[:, 1 * LANES:2 * LANES] = jnp.where(even, 0.0, pltpu.roll(kk, 32, 1)).astype(BF16)
    k4_ref[:, 2 * LANES:3 * LANES] = jnp.where(even, pltpu.roll(kk, 96, 1), 0.0).astype(BF16)
    k4_ref[:, 3 * LANES:4 * LANES] = jnp.where(even, 0.0, kk).astype(BF16)
    va_ref[...] = y[:, LANES:].T.astype(BF16)
    c0 += 2 * LANES
    d = x_ref.shape[1]
    sgr_ref[...] = jax.nn.sigmoid(proj(c0, c0 + d)).astype(BF16)
    c0 += d
    sga_ref[...] = jax.nn.sigmoid(proj(c0, c0 + d)).astype(BF16)


def _proj_call(x1, gn, w, cosr, sinr, cosa, sina, seq):
    t, d = x1.shape
    tm = TOKEN_TILE
    ret_w = RET_HEADS * RET_DK
    att_qw = ATT_Q_HEADS * ATT_HD
    n_pos = seq // tm
    tok = lambda width: pl.BlockSpec((tm, width), lambda i: (i, 0))
    tab = pl.BlockSpec((tm, LANES), lambda i: (i % n_pos, 0))
    widths = (ret_w, ret_w, ret_w, ret_w, att_qw, 4 * LANES, LANES, d, d)
    out_specs = [tok(wd) for wd in widths]
    out_shape = [jax.ShapeDtypeStruct((t, wd), BF16) for wd in widths]
    for k_out in (1, 4, 6):
        out_specs[k_out] = pl.BlockSpec((widths[k_out], tm), lambda i: (0, i))
        out_shape[k_out] = jax.ShapeDtypeStruct((widths[k_out], t), BF16)
    return pl.pallas_call(
        functools.partial(_proj_kernel, ret_w=ret_w, att_qw=att_qw),
        grid=(t // tm,),
        in_specs=[tok(d), _resident(gn.shape), _resident(w.shape), tab, tab, tab, tab],
        out_specs=out_specs,
        out_shape=out_shape,
        scratch_shapes=[pltpu.VMEM((tm, d), BF16)],
        compiler_params=_cparams(1),
        name="proj",
    )(x1, gn, w, cosr, sinr, cosa, sina)


def _ret_kernel(dec_ref, q_ref, kt_ref, v_ref, s_ref, gain_ref, z_ref, kv_ref, st_ref):
    c = RET_TILE
    dk = kt_ref.shape[0]
    n_chunks = q_ref.shape[0] // c
    h = pl.program_id(1)

    def log_gamma(direction, shape):
        return -jnp.exp(jnp.full(shape, dec_ref[direction, h], F32))

    def iota(shape, axis):
        return lax.broadcasted_iota(jnp.int32, shape, axis).astype(F32)

    rel = iota((c, c), 0) - iota((c, c), 1)
    decay = jnp.where(rel >= 0, jnp.exp(jnp.maximum(rel, 0.0) * log_gamma(0, (c, c))),
                      jnp.exp(jnp.maximum(-rel, 0.0) * log_gamma(1, (c, c))))
    key = iota((dk, c), 1)
    kdec_f = jnp.exp((c - 1 - key) * log_gamma(0, (dk, c)))
    kdec_b = jnp.exp(key * log_gamma(1, (dk, c)))
    qry = iota((c, dk), 0)
    qdec_f = jnp.exp((qry + 1.0) * log_gamma(0, (c, dk)))
    qdec_b = jnp.exp((c - qry) * log_gamma(1, (c, dk)))
    g_f = jnp.exp(c * log_gamma(0, (dk, dk)))
    g_b = jnp.exp(c * log_gamma(1, (dk, dk)))

    def span(i):
        return slice(i * c, (i + 1) * c)

    for i in range(n_chunks):
        kt = kt_ref[:, span(i)].astype(F32)
        lhs = jnp.concatenate([(kt * kdec_f).astype(BF16), (kt * kdec_b).astype(BF16)], axis=0)
        kv_ref[i] = _dot(lhs, v_ref[span(i), :])

    state = jnp.zeros((dk, dk), F32)
    for i in range(n_chunks):
        st_ref[i, :dk, :] = state.astype(BF16)
        state = g_f * state + kv_ref[i, :dk, :]
    state = jnp.zeros((dk, dk), F32)
    for i in reversed(range(n_chunks)):
        st_ref[i, dk:, :] = state.astype(BF16)
        state = g_b * state + kv_ref[i, dk:, :]

    gain = gain_ref[...]
    for i in range(n_chunks):
        q = q_ref[span(i), :]
        scores = _dot(q, kt_ref[:, span(i)])
        inner = _dot((scores * decay).astype(BF16), v_ref[span(i), :])
        q32 = q.astype(F32)
        qd = jnp.concatenate([(q32 * qdec_f).astype(BF16), (q32 * qdec_b).astype(BF16)], axis=1)
        y = inner + _dot(qd, st_ref[i])
        mu = jnp.mean(y, axis=-1, keepdims=True)
        yc = y - mu
        var = jnp.mean(yc * yc, axis=-1, keepdims=True)
        yn = yc * lax.rsqrt(var + EPS) * gain
        z_ref[span(i), :] = (yn * s_ref[span(i), :].astype(F32)).astype(BF16)


def _ret_call(dec, qr, krt, vr, sr, gain, batch, seq):
    t, w = qr.shape
    c = RET_TILE
    n_chunks = seq // c
    blk = pl.BlockSpec((seq, RET_DK), lambda b, h: (b, h))
    return pl.pallas_call(
        _ret_kernel,
        grid=(batch, RET_HEADS),
        in_specs=[pl.BlockSpec(memory_space=pltpu.SMEM), blk,
                  pl.BlockSpec((RET_DK, seq), lambda b, h: (h, b)), blk, blk,
                  pl.BlockSpec((1, RET_DK), lambda b, h: (0, h))],
        out_specs=blk,
        out_shape=jax.ShapeDtypeStruct((t, w), BF16),
        scratch_shapes=[pltpu.VMEM((n_chunks, 2 * RET_DK, RET_DK), F32),
                        pltpu.VMEM((n_chunks, 2 * RET_DK, RET_DK), BF16)],
        compiler_params=_cparams(2),
        name="retention",
    )(dec, qr, krt, vr, sr, gain)


def _att_kernel(sink_ref, qt_ref, k_ref, kp_ref, kn_ref, vt_ref, vtp_ref, vtn_ref, ot_ref):
    blk = ATT_BLOCK
    n_blocks = qt_ref.shape[1] // blk
    grp = ATT_Q_HEADS // ATT_KV_HEADS
    hd = ATT_HD
    tile = pl.program_id(1)
    neg_inf = jnp.float32(-jnp.inf)
    kpos = lax.broadcasted_iota(jnp.int32, (blk, 2 * blk), 0)
    qpos = lax.broadcasted_iota(jnp.int32, (blk, 2 * blk), 1) % blk
    prev_bias = jnp.where(kpos >= qpos, 0.0, neg_inf)
    next_bias = jnp.where(kpos <= qpos, 0.0, neg_inf)
    first_prev_bias = prev_bias + jnp.where(tile > 0, 0.0, neg_inf)
    last_next_bias = next_bias + jnp.where(tile < pl.num_programs(1) - 1, 0.0, neg_inf)
    left = lax.broadcasted_iota(jnp.int32, (1, 2 * blk), 1) < blk

    def cols(j):
        return slice(j * blk, (j + 1) * blk)

    def scores(j, g, p):
        qt = jnp.concatenate([qt_ref[cols(2 * g), cols(j)], qt_ref[cols(2 * g + 1), cols(j)]], axis=1)
        kv = cols(2 * g + p)
        kwin = jnp.concatenate([kp_ref[:, kv] if j == 0 else k_ref[cols(j - 1), kv], k_ref[cols(j), kv],
                                kn_ref[:, kv] if j == n_blocks - 1 else k_ref[cols(j + 1), kv]], axis=0)
        return _dot(kwin, qt)

    def finish(j, g, p, s):
        s_prev = s[cols(0), :] + (first_prev_bias if j == 0 else prev_bias)
        s_cur = s[cols(1), :]
        s_next = s[cols(2), :] + (last_next_bias if j == n_blocks - 1 else next_bias)
        snk = jnp.where(left, sink_ref[grp * g + p], sink_ref[grp * g + 2 + p])
        m = jnp.maximum(jnp.max(jnp.maximum(jnp.maximum(s_prev, s_cur), s_next), axis=0, keepdims=True), snk)
        e_prev, e_cur, e_next = jnp.exp(s_prev - m), jnp.exp(s_cur - m), jnp.exp(s_next - m)
        denom = jnp.sum(e_prev + e_cur + e_next, axis=0, keepdims=True) + jnp.exp(snk - m)
        et = jnp.concatenate([e_prev.astype(BF16), e_cur.astype(BF16), e_next.astype(BF16)], axis=0)
        dims = slice(g * hd, (g + 1) * hd)
        vt = jnp.concatenate([vtp_ref[dims, :] if j == 0 else vt_ref[dims, cols(j - 1)],
                              vt_ref[dims, cols(j)],
                              vtn_ref[dims, :] if j == n_blocks - 1 else vt_ref[dims, cols(j + 1)]], axis=1)
        ot = _dot(vt, et) * (1.0 / denom)
        head_a = grp * g + p
        head_b = grp * g + 2 + p
        ot_ref[head_a * hd:(head_a + 1) * hd, cols(j)] = ot[:, :blk].astype(BF16)
        ot_ref[head_b * hd:(head_b + 1) * hd, cols(j)] = ot[:, blk:].astype(BF16)

    units = [(j, g, p) for j in range(n_blocks) for g in range(ATT_KV_HEADS) for p in range(2)]
    pending = [scores(*u) for u in units[:ATT_SCORES_AHEAD]]
    for n, unit in enumerate(units):
        if n + ATT_SCORES_AHEAD < len(units):
            pending.append(scores(*units[n + ATT_SCORES_AHEAD]))
        finish(*unit, pending.pop(0))


def _att_call(sink, qat, k4, vat, batch, seq):
    w, t = qat.shape
    tq = ATT_TILE
    n_tiles = seq // tq
    per_tile = tq // ATT_BLOCK
    n_halo = seq // ATT_BLOCK

    def prev_blk(b, i):
        return b * n_halo + jnp.maximum(i * per_tile - 1, 0)

    def next_blk(b, i):
        return b * n_halo + jnp.minimum((i + 1) * per_tile, n_halo - 1)

    def dims_by_tokens(rows):
        return (pl.BlockSpec((rows, tq), lambda b, i: (0, b * n_tiles + i)),
                pl.BlockSpec((rows, ATT_BLOCK), lambda b, i: (0, prev_blk(b, i))),
                pl.BlockSpec((rows, ATT_BLOCK), lambda b, i: (0, next_blk(b, i))))

    kw = k4.shape[1]
    return pl.pallas_call(
        _att_kernel,
        grid=(batch, n_tiles),
        in_specs=[pl.BlockSpec(memory_space=pltpu.SMEM), dims_by_tokens(w)[0],
                  pl.BlockSpec((tq, kw), lambda b, i: (b * n_tiles + i, 0)),
                  pl.BlockSpec((ATT_BLOCK, kw), lambda b, i: (prev_blk(b, i), 0)),
                  pl.BlockSpec((ATT_BLOCK, kw), lambda b, i: (next_blk(b, i), 0)),
                  *dims_by_tokens(vat.shape[0])],
        out_specs=dims_by_tokens(w)[0],
        out_shape=jax.ShapeDtypeStruct((w, t), BF16),
        compiler_params=_cparams(2),
        name="attention",
    )(sink, qat, k4, k4, k4, vat, vat, vat)


def _tail_kernel(x_ref, zr_ref, oa_ref, sgr_ref, sga_ref, wro_ref, wao_ref, wout_ref,
                 gn_ref, wgu_ref, wo_ref, gfin_ref, y_ref, h_ref, a_ref):
    y_ret = _dot(zr_ref[...], wro_ref[...])
    y_att = lax.dot_general(oa_ref[...], wao_ref[...], (((0,), (0,)), ((), ())), preferred_element_type=F32)
    merged = sgr_ref[...].astype(F32) * y_ret + sga_ref[...].astype(F32) * y_att
    x2 = x_ref[...] + _dot(merged.astype(BF16), wout_ref[...])
    x3 = _swiglu_residual(x2, gn_ref, wgu_ref, wo_ref, h_ref, a_ref)
    y_ref[...] = _rms(x3, gfin_ref[...])


def _tail_call(x1, zr, oa, sgr, sga, wro, wao, wout, gn, wgu, wo, gfin):
    t, d = x1.shape
    tm = TOKEN_TILE
    ffp = wo.shape[0]
    tok = lambda width: pl.BlockSpec((tm, width), lambda i: (i, 0))
    return pl.pallas_call(
        _tail_kernel,
        grid=(t // tm,),
        in_specs=[tok(d), tok(zr.shape[1]), pl.BlockSpec((oa.shape[0], tm), lambda i: (0, i)), tok(d), tok(d),
                  _resident(wro.shape), _resident(wao.shape), _resident(wout.shape),
                  _resident(gn.shape), _resident(wgu.shape), _resident(wo.shape),
                  _resident(gfin.shape)],
        out_specs=tok(d),
        out_shape=jax.ShapeDtypeStruct((t, d), F32),
        scratch_shapes=[pltpu.VMEM((tm, d), BF16), pltpu.VMEM((tm, ffp), BF16)],
        compiler_params=_cparams(1),
        name="tail",
    )(x1, zr, oa, sgr, sga, wro, wao, wout, gn, wgu, wo, gfin)


def _prep_ffn(w_in, w_out):
    d, two_ff = w_in.shape
    ff = two_ff // 2
    ffp = -(-ff // FF_CHUNK) * FF_CHUNK
    n = ffp // FF_CHUNK
    pad = ((0, 0), (0, ffp - ff))
    g = jnp.pad(w_in[:, :ff], pad).reshape(d, n, FF_CHUNK)
    u = jnp.pad(w_in[:, ff:], pad).reshape(d, n, FF_CHUNK)
    wgu = jnp.concatenate([g, u], axis=2).reshape(d, 2 * ffp).astype(BF16)
    wo = jnp.pad(w_out, ((0, ffp - ff), (0, 0))).astype(BF16)
    return wgu, wo


def _prep_w_in(w):
    d = w.shape[0]
    ret_w = RET_HEADS * RET_DK
    qw = ATT_Q_HEADS * ATT_HD
    kvw = ATT_KV_HEADS * ATT_HD
    a0 = 4 * ret_w
    half = ATT_HD // 2
    wq = w[:, a0:a0 + qw].reshape(d, ATT_Q_HEADS // 2, 2, 2, half).transpose(0, 1, 3, 2, 4).reshape(d, qw)
    wk = w[:, a0 + qw:a0 + qw + kvw].reshape(d, ATT_KV_HEADS // 2, 2, 2, half).transpose(0, 1, 3, 2, 4).reshape(d, kvw)
    return jnp.concatenate([w[:, :a0], wq, wk, w[:, a0 + qw + kvw:]], axis=1).astype(BF16)


def _rope_tables(seq):
    pos = jnp.arange(seq, dtype=F32)[:, None]

    def cs(half):
        inv_freq = ROPE_THETA ** (-jnp.arange(half, dtype=F32) / half)
        ang = pos * inv_freq[None, :]
        return jnp.cos(ang), jnp.sin(ang)

    c, s = cs(RET_DK // 2)
    cosr = jnp.concatenate([c, c], axis=1)
    sinr = jnp.concatenate([-s, s], axis=1)
    c, s = cs(ATT_HD // 2)
    cosa = jnp.concatenate([c, c, c, c], axis=1)
    sina = jnp.concatenate([-s, -s, s, s], axis=1)
    return cosr, sinr, cosa, sina


def kernel(x, norm_ffn1, ffn1_w_in, ffn1_w_out, norm_mix, w_in, ret_decay_fwd, ret_decay_bwd,
           ret_gn_gain, w_ret_out, att_sink, w_att_out, w_out, norm_ffn2, ffn2_w_in, ffn2_w_out,
           norm_final):
    b, s, d = x.shape
    assert ffn1_w_in.shape[0] == 1, "single-layer block"
    assert s % TOKEN_TILE == 0 and s % RET_TILE == 0 and s % ATT_TILE == 0
    xt = x.reshape(b * s, d)
    wgu1, wo1 = _prep_ffn(ffn1_w_in[0], ffn1_w_out[0])
    wgu2, wo2 = _prep_ffn(ffn2_w_in[0], ffn2_w_out[0])
    cosr, sinr, cosa, sina = _rope_tables(s)

    x1 = _ffn_call(xt, norm_ffn1, wgu1, wo1)
    qr, kr, vr, sr, qa, k4, v4, sgr, sga = _proj_call(
        x1, norm_mix, _prep_w_in(w_in[0]), cosr, sinr, cosa, sina, s)
    dec = jnp.concatenate([ret_decay_fwd, ret_decay_bwd], axis=0)
    zr = _ret_call(dec, qr, kr, vr, sr, ret_gn_gain, b, s)
    oa = _att_call(att_sink[0], qa, k4, v4, b, s)
    y = _tail_call(x1, zr, oa, sgr, sga, w_ret_out[0].astype(BF16), w_att_out[0].astype(BF16),
                   w_out[0].astype(BF16), norm_ffn2, wgu2, wo2, norm_final.reshape(1, d))
    return y.reshape(b, s, d)
```

```python
import functools

import jax
import jax.numpy as jnp
import numpy as np
from jax import lax
from jax.experimental import pallas as pl
from jax.experimental.pallas import tpu as pltpu

F32 = jnp.float32
BF16 = jnp.bfloat16

EPS = 1e-6
ROPE_THETA = 10000.0
RET_HEADS = 4
RET_DK = 128
RET_TILE = 256
ATT_Q_HEADS = 8
ATT_KV_HEADS = 2
ATT_HD = 64
ATT_BLOCK = 128
ATT_TILE = 1024
ATT_SCORES_AHEAD = 3
ONES_ROWS = 16
LANES = 128
FF_CHUNK = 256
TOKEN_TILE = 512
VMEM_LIMIT = 56 * 1024 * 1024


def _cparams(n_grid):
    return pltpu.CompilerParams(dimension_semantics=("arbitrary",) * n_grid,
                                vmem_limit_bytes=VMEM_LIMIT)


def _resident(shape):
    zeros = (0,) * len(shape)
    return pl.BlockSpec(shape, lambda *_: zeros, pipeline_mode=pl.Buffered(1))


def _rms(x, g):
    y = x * lax.rsqrt(jnp.mean(x * x, axis=-1, keepdims=True) + EPS)
    return y * g


def _dot(a, b):
    return jnp.dot(a, b, preferred_element_type=F32)


def _dot_nt(a, b):
    return lax.dot_general(a, b, (((1,), (1,)), ((), ())), preferred_element_type=F32)


def _swiglu_residual(x, gn_ref, wg_ref, wu_ref, wo_ref, h_ref, a_ref):
    h_ref[...] = _rms(x, gn_ref[...]).astype(BF16)
    fc = FF_CHUNK
    for c in range(wg_ref.shape[1] // fc):
        chunk = slice(c * fc, (c + 1) * fc)
        gu = _dot(h_ref[...], jnp.concatenate([wg_ref[:, chunk], wu_ref[:, chunk]], axis=1))
        a_ref[:, chunk] = (jax.nn.silu(gu[:, :fc]) * gu[:, fc:]).astype(BF16)
    return x + 0.5 * _dot(a_ref[...], wo_ref[...])


def _ffn_kernel(x_ref, gn_ref, wg_ref, wu_ref, wo_ref, o_ref, h_ref, a_ref):
    o_ref[...] = _swiglu_residual(x_ref[...], gn_ref, wg_ref, wu_ref, wo_ref, h_ref, a_ref)


def _ffn_call(x, gn, wg, wu, wo):
    t, d = x.shape
    tm = TOKEN_TILE
    ffp = wo.shape[0]
    return pl.pallas_call(
        _ffn_kernel,
        grid=(t // tm,),
        in_specs=[pl.BlockSpec((tm, d), lambda i: (i, 0)),
                  _resident(gn.shape), _resident(wg.shape), _resident(wu.shape), _resident(wo.shape)],
        out_specs=pl.BlockSpec((tm, d), lambda i: (i, 0)),
        out_shape=jax.ShapeDtypeStruct((t, d), F32),
        scratch_shapes=[pltpu.VMEM((tm, d), BF16), pltpu.VMEM((tm, ffp), BF16)],
        compiler_params=_cparams(1),
        name="ffn1",
    )(x, gn, wg, wu, wo)


def _rope128(y, cos, sin_signed):
    return y * cos + pltpu.roll(y, 64, 1) * sin_signed


def _rope64(y, cos, sin_signed):
    first_half = (lax.broadcasted_iota(jnp.int32, y.shape, 1) // (ATT_HD // 2)) % 2 == 0
    partner = jnp.where(first_half, pltpu.roll(y, LANES - ATT_HD // 2, 1), pltpu.roll(y, ATT_HD // 2, 1))
    return y * cos + partner * sin_signed


def _proj_kernel(x_ref, gn_ref, w_ref, cosr_ref, sinr_ref, cosa_ref, sina_ref,
                 qr_ref, kr_ref, vr_ref, sr_ref, qa_ref, k4_ref, va_ref, sgr_ref, sga_ref,
                 h_ref, *, ret_w, att_qw):
    h_ref[...] = _rms(x_ref[...], gn_ref[...]).astype(BF16)

    def proj(a, b):
        return _dot(h_ref[...], w_ref[:, a:b])

    cosr, sinr = cosr_ref[...], sinr_ref[...]
    cosa, sina = cosa_ref[...], sina_ref[...]
    c0 = 0
    y = proj(c0, c0 + ret_w)
    for j in range(ret_w // LANES):
        sl = slice(j * LANES, (j + 1) * LANES)
        qr_ref[:, sl] = _rope128(y[:, sl], cosr, sinr).astype(BF16)
    c0 += ret_w
    y = proj(c0, c0 + ret_w)
    for j in range(ret_w // LANES):
        sl = slice(j * LANES, (j + 1) * LANES)
        kr_ref[sl, :] = (_rope128(y[:, sl], cosr, sinr) * (RET_DK ** -0.5)).T.astype(BF16)
    c0 += ret_w
    vr_ref[...] = proj(c0, c0 + ret_w).astype(BF16)
    c0 += ret_w
    sr_ref[...] = jax.nn.silu(proj(c0, c0 + ret_w)).astype(BF16)
    c0 += ret_w
    y = proj(c0, c0 + att_qw)
    for j in range(att_qw // LANES):
        sl = slice(j * LANES, (j + 1) * LANES)
        qa_ref[sl, :] = (_rope64(y[:, sl], cosa, sina) * (ATT_HD ** -0.5)).T.astype(BF16)
    c0 += att_qw
    y = proj(c0, c0 + 2 * LANES)
    kk = _rope64(y[:, :LANES], cosa, sina)
    low = lax.broadcasted_iota(jnp.int32, kk.shape, 1) < ATT_HD
    kswap = pltpu.roll(kk, ATT_HD, 1)
    k4_ref[:, 0 * LANES:1 * LANES] = jnp.where(low, kk, 0.0).astype(BF16)
    k4_ref[:, 1 * LANES:2 * LANES] = jnp.where(low, 0.0, kswap).astype(BF16)
    k4_ref[:, 2 * LANES:3 * LANES] = jnp.where(low, kswap, 0.0).astype(BF16)
    k4_ref[:, 3 * LANES:4 * LANES] = jnp.where(low, 0.0, kk).astype(BF16)
    va_ref[...] = y[:, LANES:].T.astype(BF16)
    c0 += 2 * LANES
    d = x_ref.shape[1]
    sgr_ref[...] = jax.nn.sigmoid(proj(c0, c0 + d)).astype(BF16)
    c0 += d
    sga_ref[...] = jax.nn.sigmoid(proj(c0, c0 + d)).astype(BF16)


def _proj_call(x1, gn, w, cosr, sinr, cosa, sina, seq):
    t, d = x1.shape
    tm = TOKEN_TILE
    ret_w = RET_HEADS * RET_DK
    att_qw = ATT_Q_HEADS * ATT_HD
    n_pos = seq // tm
    tok = lambda width: pl.BlockSpec((tm, width), lambda i: (i, 0))
    tab = pl.BlockSpec((tm, LANES), lambda i: (i % n_pos, 0))
    widths = (ret_w, ret_w, ret_w, ret_w, att_qw, 4 * LANES, LANES, d, d)
    out_specs = [tok(wd) for wd in widths]
    out_shape = [jax.ShapeDtypeStruct((t, wd), BF16) for wd in widths]
    for k_out in (1, 4, 6):
        out_specs[k_out] = pl.BlockSpec((widths[k_out], tm), lambda i: (0, i))
        out_shape[k_out] = jax.ShapeDtypeStruct((widths[k_out], t), BF16)
    return pl.pallas_call(
        functools.partial(_proj_kernel, ret_w=ret_w, att_qw=att_qw),
        grid=(t // tm,),
        in_specs=[tok(d), _resident(gn.shape), _resident(w.shape), tab, tab, tab, tab],
        out_specs=out_specs,
        out_shape=out_shape,
        scratch_shapes=[pltpu.VMEM((tm, d), BF16)],
        compiler_params=_cparams(1),
        name="proj",
    )(x1, gn, w, cosr, sinr, cosa, sina)


def _ret_kernel(dec_ref, q_ref, kt_ref, v_ref, s_ref, gain_ref, z_ref, kv_ref, st_ref):
    c = RET_TILE
    dk = kt_ref.shape[0]
    n_chunks = q_ref.shape[0] // c
    h = pl.program_id(1)

    def log_gamma(direction, shape):
        return -jnp.exp(jnp.full(shape, dec_ref[direction, h], F32))

    def iota(shape, axis):
        return lax.broadcasted_iota(jnp.int32, shape, axis).astype(F32)

    rel = iota((c, c), 0) - iota((c, c), 1)
    decay = jnp.where(rel >= 0, jnp.exp(jnp.maximum(rel, 0.0) * log_gamma(0, (c, c))),
                      jnp.exp(jnp.maximum(-rel, 0.0) * log_gamma(1, (c, c))))
    key = iota((dk, c), 1)
    kdec_f = jnp.exp((c - 1 - key) * log_gamma(0, (dk, c)))
    kdec_b = jnp.exp(key * log_gamma(1, (dk, c)))
    qry = iota((c, dk), 0)
    qdec_f = jnp.exp((qry + 1.0) * log_gamma(0, (c, dk)))
    qdec_b = jnp.exp((c - qry) * log_gamma(1, (c, dk)))
    g_f = jnp.exp(c * log_gamma(0, (dk, dk)))
    g_b = jnp.exp(c * log_gamma(1, (dk, dk)))

    def span(i):
        return slice(i * c, (i + 1) * c)

    for i in range(n_chunks):
        kt = kt_ref[:, span(i)].astype(F32)
        lhs = jnp.concatenate([(kt * kdec_f).astype(BF16), (kt * kdec_b).astype(BF16)], axis=0)
        kv_ref[i] = _dot(lhs, v_ref[span(i), :])

    state = jnp.zeros((dk, dk), F32)
    for i in range(n_chunks):
        st_ref[i, :dk, :] = state.astype(BF16)
        state = g_f * state + kv_ref[i, :dk, :]
    state = jnp.zeros((dk, dk), F32)
    for i in reversed(range(n_chunks)):
        st_ref[i, dk:, :] = state.astype(BF16)
        state = g_b * state + kv_ref[i, dk:, :]

    gain = gain_ref[...]
    for i in range(n_chunks):
        q = q_ref[span(i), :]
        scores = _dot(q, kt_ref[:, span(i)])
        inner = _dot((scores * decay).astype(BF16), v_ref[span(i), :])
        q32 = q.astype(F32)
        qd = jnp.concatenate([(q32 * qdec_f).astype(BF16), (q32 * qdec_b).astype(BF16)], axis=1)
        y = inner + _dot(qd, st_ref[i])
        mu = jnp.mean(y, axis=-1, keepdims=True)
        yc = y - mu
        var = jnp.mean(yc * yc, axis=-1, keepdims=True)
        yn = yc * lax.rsqrt(var + EPS) * gain
        z_ref[span(i), :] = (yn * s_ref[span(i), :].astype(F32)).astype(BF16)


def _ret_call(dec, qr, krt, vr, sr, gain, batch, seq):
    t, w = qr.shape
    c = RET_TILE
    n_chunks = seq // c
    blk = pl.BlockSpec((seq, RET_DK), lambda b, h: (b, h))
    return pl.pallas_call(
        _ret_kernel,
        grid=(batch, RET_HEADS),
        in_specs=[pl.BlockSpec(memory_space=pltpu.SMEM), blk,
                  pl.BlockSpec((RET_DK, seq), lambda b, h: (h, b)), blk, blk,
                  pl.BlockSpec((1, RET_DK), lambda b, h: (0, h))],
        out_specs=blk,
        out_shape=jax.ShapeDtypeStruct((t, w), BF16),
        scratch_shapes=[pltpu.VMEM((n_chunks, 2 * RET_DK, RET_DK), F32),
                        pltpu.VMEM((n_chunks, 2 * RET_DK, RET_DK), BF16)],
        compiler_params=_cparams(2),
        name="retention",
    )(dec, qr, krt, vr, sr, gain)


def _att_kernel(sink_ref, qt_ref, k_ref, kp_ref, kn_ref, vt_ref, vtp_ref, vtn_ref, ot_ref):
    blk = ATT_BLOCK
    n_blocks = qt_ref.shape[1] // blk
    grp = ATT_Q_HEADS // ATT_KV_HEADS
    hd = ATT_HD
    tile = pl.program_id(1)
    neg_inf = jnp.float32(-jnp.inf)
    kpos = lax.broadcasted_iota(jnp.int32, (blk, 2 * blk), 0)
    qpos = lax.broadcasted_iota(jnp.int32, (blk, 2 * blk), 1) % blk
    prev_bias = jnp.where(kpos >= qpos, 0.0, neg_inf)
    next_bias = jnp.where(kpos <= qpos, 0.0, neg_inf)
    first_prev_bias = prev_bias + jnp.where(tile > 0, 0.0, neg_inf)
    last_next_bias = next_bias + jnp.where(tile < pl.num_programs(1) - 1, 0.0, neg_inf)
    left = lax.broadcasted_iota(jnp.int32, (1, 2 * blk), 1) < blk

    def cols(j):
        return slice(j * blk, (j + 1) * blk)

    def scores(j, g, p):
        qt = jnp.concatenate([qt_ref[cols(2 * g), cols(j)], qt_ref[cols(2 * g + 1), cols(j)]], axis=1)
        kv = cols(2 * g + p)
        kwin = jnp.concatenate([kp_ref[:, kv] if j == 0 else k_ref[cols(j - 1), kv], k_ref[cols(j), kv],
                                kn_ref[:, kv] if j == n_blocks - 1 else k_ref[cols(j + 1), kv]], axis=0)
        return _dot(kwin, qt)

    def finish(j, g, p, s):
        s_prev = s[cols(0), :] + (first_prev_bias if j == 0 else prev_bias)
        s_cur = s[cols(1), :]
        s_next = s[cols(2), :] + (last_next_bias if j == n_blocks - 1 else next_bias)
        snk = jnp.where(left, sink_ref[grp * g + p], sink_ref[grp * g + 2 + p])
        m = jnp.maximum(jnp.max(jnp.maximum(jnp.maximum(s_prev, s_cur), s_next), axis=0, keepdims=True), snk)
        et = jnp.concatenate([jnp.exp(s_prev - m).astype(BF16), jnp.exp(s_cur - m).astype(BF16),
                              jnp.exp(s_next - m).astype(BF16)], axis=0)
        dims = slice(g * hd, (g + 1) * hd)
        vt = jnp.concatenate([vtp_ref[dims, :] if j == 0 else vt_ref[dims, cols(j - 1)],
                              vt_ref[dims, cols(j)],
                              vtn_ref[dims, :] if j == n_blocks - 1 else vt_ref[dims, cols(j + 1)]], axis=1)
        acc = _dot(jnp.concatenate([vt, jnp.ones((ONES_ROWS, 3 * blk), BF16)], axis=0), et)
        denom = acc[hd:hd + 1, :] + jnp.exp(snk - m)
        ot = acc[:hd, :] * (1.0 / denom)
        head_a = grp * g + p
        head_b = grp * g + 2 + p
        ot_ref[head_a * hd:(head_a + 1) * hd, cols(j)] = ot[:, :blk].astype(BF16)
        ot_ref[head_b * hd:(head_b + 1) * hd, cols(j)] = ot[:, blk:].astype(BF16)

    units = [(j, g, p) for j in range(n_blocks) for g in range(ATT_KV_HEADS) for p in range(2)]
    pending = [scores(*u) for u in units[:ATT_SCORES_AHEAD]]
    for n, unit in enumerate(units):
        if n + ATT_SCORES_AHEAD < len(units):
            pending.append(scores(*units[n + ATT_SCORES_AHEAD]))
        finish(*unit, pending.pop(0))


def _att_call(sink, qat, k4, vat, batch, seq):
    w, t = qat.shape
    tq = ATT_TILE
    n_tiles = seq // tq
    per_tile = tq // ATT_BLOCK
    n_halo = seq // ATT_BLOCK

    def prev_blk(b, i):
        return b * n_halo + jnp.maximum(i * per_tile - 1, 0)

    def next_blk(b, i):
        return b * n_halo + jnp.minimum((i + 1) * per_tile, n_halo - 1)

    def dims_by_tokens(rows):
        return (pl.BlockSpec((rows, tq), lambda b, i: (0, b * n_tiles + i)),
                pl.BlockSpec((rows, ATT_BLOCK), lambda b, i: (0, prev_blk(b, i))),
                pl.BlockSpec((rows, ATT_BLOCK), lambda b, i: (0, next_blk(b, i))))

    kw = k4.shape[1]
    return pl.pallas_call(
        _att_kernel,
        grid=(batch, n_tiles),
        in_specs=[pl.BlockSpec(memory_space=pltpu.SMEM), dims_by_tokens(w)[0],
                  pl.BlockSpec((tq, kw), lambda b, i: (b * n_tiles + i, 0)),
                  pl.BlockSpec((ATT_BLOCK, kw), lambda b, i: (prev_blk(b, i), 0)),
                  pl.BlockSpec((ATT_BLOCK, kw), lambda b, i: (next_blk(b, i), 0)),
                  *dims_by_tokens(vat.shape[0])],
        out_specs=dims_by_tokens(w)[0],
        out_shape=jax.ShapeDtypeStruct((w, t), BF16),
        compiler_params=_cparams(2),
        name="attention",
    )(sink, qat, k4, k4, k4, vat, vat, vat)


def _tail_kernel(x_ref, zr_ref, oa_ref, sgr_ref, sga_ref, wro_ref, wao_ref, wout_ref,
                 gn_ref, wg_ref, wu_ref, wo_ref, gfin_ref, y_ref, h_ref, a_ref):
    y_ret = _dot(zr_ref[...], wro_ref[...])
    y_att = lax.dot_general(oa_ref[...], wao_ref[...], (((0,), (0,)), ((), ())), preferred_element_type=F32)
    merged = sgr_ref[...].astype(F32) * y_ret + sga_ref[...].astype(F32) * y_att
    x2 = x_ref[...] + _dot(merged.astype(BF16), wout_ref[...])
    x3 = _swiglu_residual(x2, gn_ref, wg_ref, wu_ref, wo_ref, h_ref, a_ref)
    y_ref[...] = _rms(x3, gfin_ref[...])


def _tail_call(x1, zr, oa, sgr, sga, wro, wao, wout, gn, wg, wu, wo, gfin):
    t, d = x1.shape
    tm = TOKEN_TILE
    ffp = wo.shape[0]
    tok = lambda width: pl.BlockSpec((tm, width), lambda i: (i, 0))
    return pl.pallas_call(
        _tail_kernel,
        grid=(t // tm,),
        in_specs=[tok(d), tok(zr.shape[1]), pl.BlockSpec((oa.shape[0], tm), lambda i: (0, i)), tok(d), tok(d),
                  _resident(wro.shape), _resident(wao.shape), _resident(wout.shape),
                  _resident(gn.shape), _resident(wg.shape), _resident(wu.shape), _resident(wo.shape),
                  _resident(gfin.shape)],
        out_specs=tok(d),
        out_shape=jax.ShapeDtypeStruct((t, d), F32),
        scratch_shapes=[pltpu.VMEM((tm, d), BF16), pltpu.VMEM((tm, ffp), BF16)],
        compiler_params=_cparams(1),
        name="tail",
    )(x1, zr, oa, sgr, sga, wro, wao, wout, gn, wg, wu, wo, gfin)


def _prep_ffn(w_in, w_out):
    d, two_ff = w_in.shape
    ff = two_ff // 2
    ffp = -(-ff // FF_CHUNK) * FF_CHUNK
    pad = ((0, 0), (0, ffp - ff))
    wg = jnp.pad(w_in[:, :ff].astype(BF16), pad)
    wu = jnp.pad(w_in[:, ff:].astype(BF16), pad)
    wo = jnp.pad(w_out.astype(BF16), ((0, ffp - ff), (0, 0)))
    return wg, wu, wo


def _rope_tables(seq):
    pos = jnp.arange(seq, dtype=F32)[:, None]

    def cs(half):
        inv_freq = ROPE_THETA ** (-jnp.arange(half, dtype=F32) / half)
        ang = pos * inv_freq[None, :]
        return jnp.cos(ang), jnp.sin(ang)

    c, s = cs(RET_DK // 2)
    cosr = jnp.concatenate([c, c], axis=1)
    sinr = jnp.concatenate([-s, s], axis=1)
    c, s = cs(ATT_HD // 2)
    cosa = jnp.concatenate([c, c, c, c], axis=1)
    sina = jnp.concatenate([-s, s, -s, s], axis=1)
    return cosr, sinr, cosa, sina


def kernel(x, norm_ffn1, ffn1_w_in, ffn1_w_out, norm_mix, w_in, ret_decay_fwd, ret_decay_bwd,
           ret_gn_gain, w_ret_out, att_sink, w_att_out, w_out, norm_ffn2, ffn2_w_in, ffn2_w_out,
           norm_final):
    b, s, d = x.shape
    assert ffn1_w_in.shape[0] == 1, "single-layer block"
    assert s % TOKEN_TILE == 0 and s % RET_TILE == 0 and s % ATT_TILE == 0
    xt = x.reshape(b * s, d)
    ffn1_w = _prep_ffn(ffn1_w_in[0], ffn1_w_out[0])
    ffn2_w = _prep_ffn(ffn2_w_in[0], ffn2_w_out[0])
    cosr, sinr, cosa, sina = _rope_tables(s)

    x1 = _ffn_call(xt, norm_ffn1, *ffn1_w)
    qr, krt, vr, sr, qat, k4, vat, sgr, sga = _proj_call(
        x1, norm_mix, w_in[0].astype(BF16), cosr, sinr, cosa, sina, s)
    dec = jnp.concatenate([ret_decay_fwd, ret_decay_bwd], axis=0)
    zr = _ret_call(dec, qr, krt, vr, sr, ret_gn_gain, b, s)
    oat = _att_call(att_sink[0], qat, k4, vat, b, s)
    y = _tail_call(x1, zr, oat, sgr, sga, w_ret_out[0].astype(BF16), w_att_out[0].astype(BF16),
                   w_out[0].astype(BF16), norm_ffn2, *ffn2_w, norm_final.reshape(1, d))
    return y.reshape(b, s, d)
```

```python
import functools

import jax
import jax.numpy as jnp
import numpy as np
from jax import lax
from jax.experimental import pallas as pl
from jax.experimental.pallas import tpu as pltpu

F32 = jnp.float32
BF16 = jnp.bfloat16

EPS = 1e-6
LOG2_E = 1.4426950408889634
ROPE_THETA = 10000.0
RET_HEADS = 4
RET_DK = 128
RET_TILE = 256
ATT_Q_HEADS = 8
ATT_KV_HEADS = 2
ATT_HD = 64
ATT_BLOCK = 128
ATT_TILE = 1024
ATT_SCORES_AHEAD = 3
ONES_ROWS = 16
LANES = 128
FF_CHUNK = 256
TOKEN_TILE = 1024
TAIL_FFN_ROWS = 512
FFN_TILE = 1024
PROJ_TILE = 1024
VMEM_LIMIT = 58 * 1024 * 1024


def _cparams(n_grid):
    return pltpu.CompilerParams(dimension_semantics=("arbitrary",) * n_grid,
                                vmem_limit_bytes=VMEM_LIMIT)


def _resident(shape):
    zeros = (0,) * len(shape)
    return pl.BlockSpec(shape, lambda *_: zeros, pipeline_mode=pl.Buffered(1))


def _rms(x, g):
    y = x * lax.rsqrt(jnp.mean(x * x, axis=-1, keepdims=True) + EPS)
    return y * g


def _dot(a, b):
    return jnp.dot(a, b, preferred_element_type=F32)


def _dot_nt(a, b):
    return lax.dot_general(a, b, (((1,), (1,)), ((), ())), preferred_element_type=F32)


def _swiglu_residual(x, gn_ref, wg_ref, wu_ref, wo_ref, h_ref, a_ref):
    h_ref[...] = _rms(x, gn_ref[...]).astype(BF16)
    fc = FF_CHUNK
    for c in range(wg_ref.shape[1] // fc):
        chunk = slice(c * fc, (c + 1) * fc)
        gu = _dot(h_ref[...], jnp.concatenate([wg_ref[:, chunk], wu_ref[:, chunk]], axis=1))
        a_ref[:, chunk] = (jax.nn.silu(gu[:, :fc]) * gu[:, fc:]).astype(BF16)
    return x + 0.5 * _dot(a_ref[...], wo_ref[...])


def _ffn_kernel(x_ref, gn_ref, wg_ref, wu_ref, wo_ref, o_ref, h_ref, a_ref):
    o_ref[...] = _swiglu_residual(x_ref[...], gn_ref, wg_ref, wu_ref, wo_ref, h_ref, a_ref)


def _ffn_call(x, gn, wg, wu, wo):
    t, d = x.shape
    tm = FFN_TILE
    ffp = wo.shape[0]
    return pl.pallas_call(
        _ffn_kernel,
        grid=(t // tm,),
        in_specs=[pl.BlockSpec((tm, d), lambda i: (i, 0)),
                  _resident(gn.shape), _resident(wg.shape), _resident(wu.shape), _resident(wo.shape)],
        out_specs=pl.BlockSpec((tm, d), lambda i: (i, 0)),
        out_shape=jax.ShapeDtypeStruct((t, d), F32),
        scratch_shapes=[pltpu.VMEM((tm, d), BF16), pltpu.VMEM((tm, ffp), BF16)],
        compiler_params=_cparams(1),
        name="ffn1",
    )(x, gn, wg, wu, wo)


def _rope128(y, cos, sin_signed):
    return y * cos + pltpu.roll(y, 64, 1) * sin_signed


def _rope64(y, cos, sin_signed):
    first_half = (lax.broadcasted_iota(jnp.int32, y.shape, 1) // (ATT_HD // 2)) % 2 == 0
    partner = jnp.where(first_half, pltpu.roll(y, LANES - ATT_HD // 2, 1), pltpu.roll(y, ATT_HD // 2, 1))
    return y * cos + partner * sin_signed


def _proj_kernel(x_ref, gn_ref, w_ref, cosr_ref, sinr_ref, cosa_ref, sina_ref,
                 qr_ref, kr_ref, vr_ref, sr_ref, qa_ref, k4_ref, va_ref, sgr_ref, sga_ref,
                 h_ref, *, ret_w, att_qw):
    h_ref[...] = _rms(x_ref[...], gn_ref[...]).astype(BF16)

    def proj(a, b):
        return _dot(h_ref[...], w_ref[:, a:b])

    cosr, sinr = cosr_ref[...], sinr_ref[...]
    cosa, sina = cosa_ref[...], sina_ref[...]
    c0 = 0
    y = proj(c0, c0 + ret_w)
    for j in range(ret_w // LANES):
        sl = slice(j * LANES, (j + 1) * LANES)
        qr_ref[:, sl] = _rope128(y[:, sl], cosr, sinr).astype(BF16)
    c0 += ret_w
    y = proj(c0, c0 + ret_w)
    for j in range(ret_w // LANES):
        sl = slice(j * LANES, (j + 1) * LANES)
        kr_ref[sl, :] = (_rope128(y[:, sl], cosr, sinr) * (RET_DK ** -0.5)).T.astype(BF16)
    c0 += ret_w
    vr_ref[...] = proj(c0, c0 + ret_w).astype(BF16)
    c0 += ret_w
    sr_ref[...] = jax.nn.silu(proj(c0, c0 + ret_w)).astype(BF16)
    c0 += ret_w
    y = proj(c0, c0 + att_qw)
    for j in range(att_qw // LANES):
        sl = slice(j * LANES, (j + 1) * LANES)
        qa_ref[sl, :] = (_rope64(y[:, sl], cosa, sina) * (ATT_HD ** -0.5 * LOG2_E)).T.astype(BF16)
    c0 += att_qw
    y = proj(c0, c0 + 2 * LANES)
    kk = _rope64(y[:, :LANES], cosa, sina)
    low = lax.broadcasted_iota(jnp.int32, kk.shape, 1) < ATT_HD
    kswap = pltpu.roll(kk, ATT_HD, 1)
    k4_ref[:, 0 * LANES:1 * LANES] = jnp.where(low, kk, 0.0).astype(BF16)
    k4_ref[:, 1 * LANES:2 * LANES] = jnp.where(low, 0.0, kswap).astype(BF16)
    k4_ref[:, 2 * LANES:3 * LANES] = jnp.where(low, kswap, 0.0).astype(BF16)
    k4_ref[:, 3 * LANES:4 * LANES] = jnp.where(low, 0.0, kk).astype(BF16)
    va_ref[...] = y[:, LANES:].T.astype(BF16)
    c0 += 2 * LANES
    d = x_ref.shape[1]
    sgr_ref[...] = jax.nn.sigmoid(proj(c0, c0 + d)).astype(BF16)
    c0 += d
    sga_ref[...] = jax.nn.sigmoid(proj(c0, c0 + d)).astype(BF16)


def _proj_call(x1, gn, w, cosr, sinr, cosa, sina, seq):
    t, d = x1.shape
    tm = PROJ_TILE
    ret_w = RET_HEADS * RET_DK
    att_qw = ATT_Q_HEADS * ATT_HD
    n_pos = seq // tm
    tok = lambda width: pl.BlockSpec((tm, width), lambda i: (i, 0))
    tab = pl.BlockSpec((tm, LANES), lambda i: (i % n_pos, 0))
    widths = (ret_w, ret_w, ret_w, ret_w, att_qw, 4 * LANES, LANES, d, d)
    out_specs = [tok(wd) for wd in widths]
    out_shape = [jax.ShapeDtypeStruct((t, wd), BF16) for wd in widths]
    for k_out in (1, 4, 6):
        out_specs[k_out] = pl.BlockSpec((widths[k_out], tm), lambda i: (0, i))
        out_shape[k_out] = jax.ShapeDtypeStruct((widths[k_out], t), BF16)
    return pl.pallas_call(
        functools.partial(_proj_kernel, ret_w=ret_w, att_qw=att_qw),
        grid=(t // tm,),
        in_specs=[tok(d), _resident(gn.shape), _resident(w.shape), tab, tab, tab, tab],
        out_specs=out_specs,
        out_shape=out_shape,
        scratch_shapes=[pltpu.VMEM((tm, d), BF16)],
        compiler_params=_cparams(1),
        name="proj",
    )(x1, gn, w, cosr, sinr, cosa, sina)


def _ret_kernel(dec_ref, q_ref, kt_ref, v_ref, s_ref, gain_ref, z_ref, kv_ref, st_ref):
    c = RET_TILE
    dk = kt_ref.shape[0]
    n_chunks = q_ref.shape[0] // c
    h = pl.program_id(1)

    def log_gamma(direction, shape):
        return -jnp.exp(jnp.full(shape, dec_ref[direction, h], F32))

    def iota(shape, axis):
        return lax.broadcasted_iota(jnp.int32, shape, axis).astype(F32)

    rel = iota((c, c), 0) - iota((c, c), 1)
    decay = jnp.where(rel >= 0, jnp.exp(jnp.maximum(rel, 0.0) * log_gamma(0, (c, c))),
                      jnp.exp(jnp.maximum(-rel, 0.0) * log_gamma(1, (c, c))))
    key = iota((dk, c), 1)
    kdec_f = jnp.exp((c - 1 - key) * log_gamma(0, (dk, c)))
    kdec_b = jnp.exp(key * log_gamma(1, (dk, c)))
    qry = iota((c, dk), 0)
    qdec_f = jnp.exp((qry + 1.0) * log_gamma(0, (c, dk)))
    qdec_b = jnp.exp((c - qry) * log_gamma(1, (c, dk)))
    g_f = jnp.exp(c * log_gamma(0, (dk, dk)))
    g_b = jnp.exp(c * log_gamma(1, (dk, dk)))

    def span(i):
        return slice(i * c, (i + 1) * c)

    for i in range(n_chunks):
        kt = kt_ref[:, span(i)].astype(F32)
        lhs = jnp.concatenate([(kt * kdec_f).astype(BF16), (kt * kdec_b).astype(BF16)], axis=0)
        kv_ref[i] = _dot(lhs, v_ref[span(i), :])

    state = jnp.zeros((dk, dk), F32)
    for i in range(n_chunks):
        st_ref[i, :dk, :] = state.astype(BF16)
        state = g_f * state + kv_ref[i, :dk, :]
    state = jnp.zeros((dk, dk), F32)
    for i in reversed(range(n_chunks)):
        st_ref[i, dk:, :] = state.astype(BF16)
        state = g_b * state + kv_ref[i, dk:, :]

    gain = gain_ref[...]
    for i in range(n_chunks):
        q = q_ref[span(i), :]
        scores = _dot(q, kt_ref[:, span(i)])
        inner = _dot((scores * decay).astype(BF16), v_ref[span(i), :])
        q32 = q.astype(F32)
        qd = jnp.concatenate([(q32 * qdec_f).astype(BF16), (q32 * qdec_b).astype(BF16)], axis=1)
        y = inner + _dot(qd, st_ref[i])
        mu = jnp.mean(y, axis=-1, keepdims=True)
        yc = y - mu
        var = jnp.mean(yc * yc, axis=-1, keepdims=True)
        yn = yc * lax.rsqrt(var + EPS) * gain
        z_ref[span(i), :] = (yn * s_ref[span(i), :].astype(F32)).astype(BF16)


def _ret_call(dec, qr, krt, vr, sr, gain, batch, seq):
    t, w = qr.shape
    c = RET_TILE
    n_chunks = seq // c
    blk = pl.BlockSpec((seq, RET_DK), lambda b, h: (b, h))
    return pl.pallas_call(
        _ret_kernel,
        grid=(batch, RET_HEADS),
        in_specs=[pl.BlockSpec(memory_space=pltpu.SMEM), blk,
                  pl.BlockSpec((RET_DK, seq), lambda b, h: (h, b)), blk, blk,
                  pl.BlockSpec((1, RET_DK), lambda b, h: (0, h))],
        out_specs=blk,
        out_shape=jax.ShapeDtypeStruct((t, w), BF16),
        scratch_shapes=[pltpu.VMEM((n_chunks, 2 * RET_DK, RET_DK), F32),
                        pltpu.VMEM((n_chunks, 2 * RET_DK, RET_DK), BF16)],
        compiler_params=_cparams(2),
        name="retention",
    )(dec, qr, krt, vr, sr, gain)


def _att_kernel(sink_ref, qt_ref, k_ref, kp_ref, kn_ref, vt_ref, vtp_ref, vtn_ref, ot_ref):
    blk = ATT_BLOCK
    n_blocks = qt_ref.shape[1] // blk
    grp = ATT_Q_HEADS // ATT_KV_HEADS
    hd = ATT_HD
    tile = pl.program_id(1)
    neg_inf = jnp.float32(-jnp.inf)
    kpos = lax.broadcasted_iota(jnp.int32, (blk, 2 * blk), 0)
    qpos = lax.broadcasted_iota(jnp.int32, (blk, 2 * blk), 1) % blk
    prev_bias = jnp.where(kpos >= qpos, 0.0, neg_inf)
    next_bias = jnp.where(kpos <= qpos, 0.0, neg_inf)
    first_prev_bias = prev_bias + jnp.where(tile > 0, 0.0, neg_inf)
    last_next_bias = next_bias + jnp.where(tile < pl.num_programs(1) - 1, 0.0, neg_inf)
    left = lax.broadcasted_iota(jnp.int32, (1, 2 * blk), 1) < blk

    def cols(j):
        return slice(j * blk, (j + 1) * blk)

    def scores(j, g, p):
        qt = jnp.concatenate([qt_ref[cols(2 * g), cols(j)], qt_ref[cols(2 * g + 1), cols(j)]], axis=1)
        kv = cols(2 * g + p)
        kwin = jnp.concatenate([kp_ref[:, kv] if j == 0 else k_ref[cols(j - 1), kv], k_ref[cols(j), kv],
                                kn_ref[:, kv] if j == n_blocks - 1 else k_ref[cols(j + 1), kv]], axis=0)
        return _dot(kwin, qt)

    def finish(j, g, p, s):
        s_prev = s[cols(0), :] + (first_prev_bias if j == 0 else prev_bias)
        s_cur = s[cols(1), :]
        s_next = s[cols(2), :] + (last_next_bias if j == n_blocks - 1 else next_bias)
        snk = jnp.where(left, sink_ref[grp * g + p], sink_ref[grp * g + 2 + p]) * LOG2_E
        m = jnp.maximum(jnp.max(jnp.maximum(jnp.maximum(s_prev, s_cur), s_next), axis=0, keepdims=True), snk)
        et = jnp.concatenate([jnp.exp2(s_prev - m).astype(BF16), jnp.exp2(s_cur - m).astype(BF16),
                              jnp.exp2(s_next - m).astype(BF16)], axis=0)
        dims = slice(g * hd, (g + 1) * hd)
        vt = jnp.concatenate([vtp_ref[dims, :] if j == 0 else vt_ref[dims, cols(j - 1)],
                              vt_ref[dims, cols(j)],
                              vtn_ref[dims, :] if j == n_blocks - 1 else vt_ref[dims, cols(j + 1)]], axis=1)
        acc = _dot(jnp.concatenate([vt, jnp.ones((ONES_ROWS, 3 * blk), BF16)], axis=0), et)
        denom = acc[hd:hd + 1, :] + jnp.exp2(snk - m)
        ot = acc[:hd, :] * (1.0 / denom)
        head_a = grp * g + p
        head_b = grp * g + 2 + p
        ot_ref[head_a * hd:(head_a + 1) * hd, cols(j)] = ot[:, :blk].astype(BF16)
        ot_ref[head_b * hd:(head_b + 1) * hd, cols(j)] = ot[:, blk:].astype(BF16)

    units = [(j, g, p) for j in range(n_blocks) for g in range(ATT_KV_HEADS) for p in range(2)]
    pending = [scores(*u) for u in units[:ATT_SCORES_AHEAD]]
    for n, unit in enumerate(units):
        if n + ATT_SCORES_AHEAD < len(units):
            pending.append(scores(*units[n + ATT_SCORES_AHEAD]))
        finish(*unit, pending.pop(0))


def _att_call(sink, qat, k4, vat, batch, seq):
    w, t = qat.shape
    tq = ATT_TILE
    n_tiles = seq // tq
    per_tile = tq // ATT_BLOCK
    n_halo = seq // ATT_BLOCK

    def prev_blk(b, i):
        return b * n_halo + jnp.maximum(i * per_tile - 1, 0)

    def next_blk(b, i):
        return b * n_halo + jnp.minimum((i + 1) * per_tile, n_halo - 1)

    def dims_by_tokens(rows):
        return (pl.BlockSpec((rows, tq), lambda b, i: (0, b * n_tiles + i)),
                pl.BlockSpec((rows, ATT_BLOCK), lambda b, i: (0, prev_blk(b, i))),
                pl.BlockSpec((rows, ATT_BLOCK), lambda b, i: (0, next_blk(b, i))))

    kw = k4.shape[1]
    return pl.pallas_call(
        _att_kernel,
        grid=(batch, n_tiles),
        in_specs=[pl.BlockSpec(memory_space=pltpu.SMEM), dims_by_tokens(w)[0],
                  pl.BlockSpec((tq, kw), lambda b, i: (b * n_tiles + i, 0)),
                  pl.BlockSpec((ATT_BLOCK, kw), lambda b, i: (prev_blk(b, i), 0)),
                  pl.BlockSpec((ATT_BLOCK, kw), lambda b, i: (next_blk(b, i), 0)),
                  *dims_by_tokens(vat.shape[0])],
        out_specs=dims_by_tokens(w)[0],
        out_shape=jax.ShapeDtypeStruct((w, t), BF16),
        compiler_params=_cparams(2),
        name="attention",
    )(sink, qat, k4, k4, k4, vat, vat, vat)


def _tail_kernel(x_ref, zr_ref, oa_ref, sgr_ref, sga_ref, wro_ref, wao_ref, wout_ref,
                 gn_ref, wg_ref, wu_ref, wo_ref, gfin_ref, y_ref, h_ref, a_ref):
    y_ret = _dot(zr_ref[...], wro_ref[...])
    y_att = lax.dot_general(oa_ref[...], wao_ref[...], (((0,), (0,)), ((), ())), preferred_element_type=F32)
    merged = sgr_ref[...].astype(F32) * y_ret + sga_ref[...].astype(F32) * y_att
    x2 = x_ref[...] + _dot(merged.astype(BF16), wout_ref[...])
    group = h_ref.shape[0]
    for r in range(x2.shape[0] // group):
        rows = slice(r * group, (r + 1) * group)
        x3 = _swiglu_residual(x2[rows, :], gn_ref, wg_ref, wu_ref, wo_ref, h_ref, a_ref)
        y_ref[rows, :] = _rms(x3, gfin_ref[...])


def _tail_call(x1, zr, oa, sgr, sga, wro, wao, wout, gn, wg, wu, wo, gfin):
    t, d = x1.shape
    tm = TOKEN_TILE
    ffp = wo.shape[0]
    tok = lambda width: pl.BlockSpec((tm, width), lambda i: (i, 0))
    return pl.pallas_call(
        _tail_kernel,
        grid=(t // tm,),
        in_specs=[tok(d), tok(zr.shape[1]), pl.BlockSpec((oa.shape[0], tm), lambda i: (0, i)), tok(d), tok(d),
                  _resident(wro.shape), _resident(wao.shape), _resident(wout.shape),
                  _resident(gn.shape), _resident(wg.shape), _resident(wu.shape), _resident(wo.shape),
                  _resident(gfin.shape)],
        out_specs=tok(d),
        out_shape=jax.ShapeDtypeStruct((t, d), F32),
        scratch_shapes=[pltpu.VMEM((TAIL_FFN_ROWS, d), BF16), pltpu.VMEM((TAIL_FFN_ROWS, ffp), BF16)],
        compiler_params=_cparams(1),
        name="tail",
    )(x1, zr, oa, sgr, sga, wro, wao, wout, gn, wg, wu, wo, gfin)


def _prep_ffn(w_in, w_out):
    d, two_ff = w_in.shape
    ff = two_ff // 2
    ffp = -(-ff // FF_CHUNK) * FF_CHUNK
    pad = ((0, 0), (0, ffp - ff))
    wg = jnp.pad(w_in[:, :ff].astype(BF16), pad)
    wu = jnp.pad(w_in[:, ff:].astype(BF16), pad)
    wo = jnp.pad(w_out.astype(BF16), ((0, ffp - ff), (0, 0)))
    return wg, wu, wo


def _rope_tables(seq):
    pos = np.arange(seq, dtype=np.float32)[:, None]

    def cs(half):
        inv_freq = np.float32(ROPE_THETA) ** (-np.arange(half, dtype=np.float32) / np.float32(half))
        ang = (pos * inv_freq[None, :].astype(np.float32)).astype(np.float64)
        return np.cos(ang).astype(np.float32), np.sin(ang).astype(np.float32)

    c, s = cs(RET_DK // 2)
    cosr = np.concatenate([c, c], axis=1)
    sinr = np.concatenate([-s, s], axis=1)
    c, s = cs(ATT_HD // 2)
    cosa = np.concatenate([c, c, c, c], axis=1)
    sina = np.concatenate([-s, s, -s, s], axis=1)
    return tuple(jnp.asarray(tab) for tab in (cosr, sinr, cosa, sina))


def kernel(x, norm_ffn1, ffn1_w_in, ffn1_w_out, norm_mix, w_in, ret_decay_fwd, ret_decay_bwd,
           ret_gn_gain, w_ret_out, att_sink, w_att_out, w_out, norm_ffn2, ffn2_w_in, ffn2_w_out,
           norm_final):
    b, s, d = x.shape
    assert ffn1_w_in.shape[0] == 1, "single-layer block"
    assert all(s % tile == 0 for tile in (TOKEN_TILE, FFN_TILE, PROJ_TILE, RET_TILE, ATT_TILE))
    xt = x.reshape(b * s, d)
    ffn1_w = _prep_ffn(ffn1_w_in[0], ffn1_w_out[0])
    ffn2_w = _prep_ffn(ffn2_w_in[0], ffn2_w_out[0])
    cosr, sinr, cosa, sina = _rope_tables(s)

    x1 = _ffn_call(xt, norm_ffn1, *ffn1_w)
    qr, krt, vr, sr, qat, k4, vat, sgr, sga = _proj_call(
        x1, norm_mix, w_in[0].astype(BF16), cosr, sinr, cosa, sina, s)
    dec = jnp.concatenate([ret_decay_fwd, ret_decay_bwd], axis=0)
    zr = _ret_call(dec, qr, krt, vr, sr, ret_gn_gain, b, s)
    oat = _att_call(att_sink[0], qat, k4, vat, b, s)
    y = _tail_call(x1, zr, oat, sgr, sga, w_ret_out[0].astype(BF16), w_att_out[0].astype(BF16),
                   w_out[0].astype(BF16), norm_ffn2, *ffn2_w, norm_final.reshape(1, d))
    return y.reshape(b, s, d)
```

```python
import functools

import jax
import jax.numpy as jnp
import numpy as np
from jax import lax
from jax.experimental import pallas as pl
from jax.experimental.pallas import tpu as pltpu

F32 = jnp.float32
BF16 = jnp.bfloat16

EPS = 1e-6
LOG2_E = 1.4426950408889634
ROPE_THETA = 10000.0
RET_HEADS = 4
RET_DK = 128
RET_TILE = 256
ATT_Q_HEADS = 8
ATT_KV_HEADS = 2
ATT_HD = 64
ATT_BLOCK = 128
ATT_TILE = 1024
ATT_SCORES_AHEAD = 3
ONES_ROWS = 16
LANES = 128
FF_CHUNK = 256
TOKEN_TILE = 1024
TAIL_FFN_ROWS = 512
FFN_TILE = 1024
PROJ_TILE = 1024
VMEM_LIMIT = 58 * 1024 * 1024


def _cparams(n_grid):
    return pltpu.CompilerParams(dimension_semantics=("arbitrary",) * n_grid,
                                vmem_limit_bytes=VMEM_LIMIT)


def _resident(shape):
    zeros = (0,) * len(shape)
    return pl.BlockSpec(shape, lambda *_: zeros, pipeline_mode=pl.Buffered(1))


def _rms(x, g):
    y = x * lax.rsqrt(jnp.mean(x * x, axis=-1, keepdims=True) + EPS)
    return y * g


def _dot(a, b):
    return jnp.dot(a, b, preferred_element_type=F32)


def _dot_nt(a, b):
    return lax.dot_general(a, b, (((1,), (1,)), ((), ())), preferred_element_type=F32)


def _swiglu_residual(x, gn_ref, wg_ref, wu_ref, wo_ref, h_ref, a_ref):
    h_ref[...] = _rms(x, gn_ref[...]).astype(BF16)
    fc = FF_CHUNK
    for c in range(wg_ref.shape[1] // fc):
        chunk = slice(c * fc, (c + 1) * fc)
        gu = _dot(h_ref[...], jnp.concatenate([wg_ref[:, chunk], wu_ref[:, chunk]], axis=1))
        a_ref[:, chunk] = (jax.nn.silu(gu[:, :fc]) * gu[:, fc:]).astype(BF16)
    return x + 0.5 * _dot(a_ref[...], wo_ref[...])


def _ffn_kernel(x_ref, gn_ref, wg_ref, wu_ref, wo_ref, o_ref, h_ref, a_ref):
    o_ref[...] = _swiglu_residual(x_ref[...], gn_ref, wg_ref, wu_ref, wo_ref, h_ref, a_ref)


def _ffn_call(x, gn, wg, wu, wo):
    t, d = x.shape
    tm = FFN_TILE
    ffp = wo.shape[0]
    return pl.pallas_call(
        _ffn_kernel,
        grid=(t // tm,),
        in_specs=[pl.BlockSpec((tm, d), lambda i: (i, 0)),
                  _resident(gn.shape), _resident(wg.shape), _resident(wu.shape), _resident(wo.shape)],
        out_specs=pl.BlockSpec((tm, d), lambda i: (i, 0)),
        out_shape=jax.ShapeDtypeStruct((t, d), F32),
        scratch_shapes=[pltpu.VMEM((tm, d), BF16), pltpu.VMEM((tm, ffp), BF16)],
        compiler_params=_cparams(1),
        name="ffn1",
    )(x, gn, wg, wu, wo)


def _rope128(y, cos, sin_signed):
    return y * cos + pltpu.roll(y, 64, 1) * sin_signed


def _rope64(y, cos, sin_signed):
    first_half = (lax.broadcasted_iota(jnp.int32, y.shape, 1) // (ATT_HD // 2)) % 2 == 0
    partner = jnp.where(first_half, pltpu.roll(y, LANES - ATT_HD // 2, 1), pltpu.roll(y, ATT_HD // 2, 1))
    return y * cos + partner * sin_signed


def _proj_kernel(x_ref, gn_ref, w_ref, cosr_ref, sinr_ref, cosa_ref, sina_ref,
                 qr_ref, kr_ref, vr_ref, sr_ref, qa_ref, k4_ref, va_ref, sgr_ref, sga_ref,
                 h_ref, *, ret_w, att_qw):
    h_ref[...] = _rms(x_ref[...], gn_ref[...]).astype(BF16)

    def proj(a, b):
        return _dot(h_ref[...], w_ref[:, a:b])

    cosr, sinr = cosr_ref[...], sinr_ref[...]
    cosa, sina = cosa_ref[...], sina_ref[...]
    c0 = 0
    y = proj(c0, c0 + ret_w)
    for j in range(ret_w // LANES):
        sl = slice(j * LANES, (j + 1) * LANES)
        qr_ref[:, sl] = _rope128(y[:, sl], cosr, sinr).astype(BF16)
    c0 += ret_w
    y = proj(c0, c0 + ret_w)
    for j in range(ret_w // LANES):
        sl = slice(j * LANES, (j + 1) * LANES)
        kr_ref[sl, :] = (_rope128(y[:, sl], cosr, sinr) * (RET_DK ** -0.5)).T.astype(BF16)
    c0 += ret_w
    vr_ref[...] = proj(c0, c0 + ret_w).astype(BF16)
    c0 += ret_w
    sr_ref[...] = jax.nn.silu(proj(c0, c0 + ret_w)).astype(BF16)
    c0 += ret_w
    y = proj(c0, c0 + att_qw)
    for j in range(att_qw // LANES):
        sl = slice(j * LANES, (j + 1) * LANES)
        qa_ref[sl, :] = (_rope64(y[:, sl], cosa, sina) * (ATT_HD ** -0.5 * LOG2_E)).T.astype(BF16)
    c0 += att_qw
    y = proj(c0, c0 + 2 * LANES)
    kk = _rope64(y[:, :LANES], cosa, sina)
    low = lax.broadcasted_iota(jnp.int32, kk.shape, 1) < ATT_HD
    kswap = pltpu.roll(kk, ATT_HD, 1)
    k4_ref[:, 0 * LANES:1 * LANES] = jnp.where(low, kk, 0.0).astype(BF16)
    k4_ref[:, 1 * LANES:2 * LANES] = jnp.where(low, 0.0, kswap).astype(BF16)
    k4_ref[:, 2 * LANES:3 * LANES] = jnp.where(low, kswap, 0.0).astype(BF16)
    k4_ref[:, 3 * LANES:4 * LANES] = jnp.where(low, 0.0, kk).astype(BF16)
    va_ref[...] = y[:, LANES:].T.astype(BF16)
    c0 += 2 * LANES
    d = x_ref.shape[1]
    sgr_ref[...] = jax.nn.sigmoid(proj(c0, c0 + d)).astype(BF16)
    c0 += d
    sga_ref[...] = jax.nn.sigmoid(proj(c0, c0 + d)).astype(BF16)


def _proj_call(x1, gn, w, cosr, sinr, cosa, sina, seq):
    t, d = x1.shape
    tm = PROJ_TILE
    ret_w = RET_HEADS * RET_DK
    att_qw = ATT_Q_HEADS * ATT_HD
    n_pos = seq // tm
    tok = lambda width: pl.BlockSpec((tm, width), lambda i: (i, 0))
    tab = pl.BlockSpec((tm, LANES), lambda i: (i % n_pos, 0))
    widths = (ret_w, ret_w, ret_w, ret_w, att_qw, 4 * LANES, LANES, d, d)
    out_specs = [tok(wd) for wd in widths]
    out_shape = [jax.ShapeDtypeStruct((t, wd), BF16) for wd in widths]
    for k_out in (1, 4, 6):
        out_specs[k_out] = pl.BlockSpec((widths[k_out], tm), lambda i: (0, i))
        out_shape[k_out] = jax.ShapeDtypeStruct((widths[k_out], t), BF16)
    return pl.pallas_call(
        functools.partial(_proj_kernel, ret_w=ret_w, att_qw=att_qw),
        grid=(t // tm,),
        in_specs=[tok(d), _resident(gn.shape), _resident(w.shape), tab, tab, tab, tab],
        out_specs=out_specs,
        out_shape=out_shape,
        scratch_shapes=[pltpu.VMEM((tm, d), BF16)],
        compiler_params=_cparams(1),
        name="proj",
    )(x1, gn, w, cosr, sinr, cosa, sina)


def _ret_stages(h, dec_ref, q_ref, kt_ref, v_ref, s_ref, gain_ref, z_ref, kv_ref, st_ref):
    c = RET_TILE
    dk = kt_ref.shape[0]
    n_chunks = q_ref.shape[0] // c

    def log_gamma(direction, shape):
        return -jnp.exp(jnp.full(shape, dec_ref[direction, h], F32))

    def iota(shape, axis):
        return lax.broadcasted_iota(jnp.int32, shape, axis).astype(F32)

    rel = iota((c, c), 0) - iota((c, c), 1)
    decay = jnp.where(rel >= 0, jnp.exp(jnp.maximum(rel, 0.0) * log_gamma(0, (c, c))),
                      jnp.exp(jnp.maximum(-rel, 0.0) * log_gamma(1, (c, c))))
    pos = iota((c, dk), 0)
    kdec_f = jnp.exp((c - 1 - pos) * log_gamma(0, (c, dk)))
    kdec_b = jnp.exp(pos * log_gamma(1, (c, dk)))
    qdec_f = jnp.exp((pos + 1.0) * log_gamma(0, (c, dk)))
    qdec_b = jnp.exp((c - pos) * log_gamma(1, (c, dk)))
    g_f = jnp.exp(c * log_gamma(0, (dk, dk)))
    g_b = jnp.exp(c * log_gamma(1, (dk, dk)))

    def span(i):
        return slice(i * c, (i + 1) * c)

    def kv_stage(i):
        v = v_ref[span(i), :].astype(F32)
        rhs = jnp.concatenate([(v * kdec_f).astype(BF16), (v * kdec_b).astype(BF16)], axis=1)
        kv_ref[i] = _dot(kt_ref[:, span(i)], rhs)

    def state_stage():
        state = jnp.zeros((dk, dk), F32)
        for i in range(n_chunks):
            st_ref[i, :, :dk] = state.astype(BF16)
            state = g_f * state + kv_ref[i, :, :dk]
        state = jnp.zeros((dk, dk), F32)
        for i in reversed(range(n_chunks)):
            st_ref[i, :, dk:] = state.astype(BF16)
            state = g_b * state + kv_ref[i, :, dk:]

    def out_stage(i):
        q = q_ref[span(i), :]
        scores = _dot(q, kt_ref[:, span(i)])
        inner = _dot((scores * decay).astype(BF16), v_ref[span(i), :])
        cross = _dot(q, st_ref[i])
        y = inner + qdec_f * cross[:, :dk] + qdec_b * cross[:, dk:]
        mu = jnp.mean(y, axis=-1, keepdims=True)
        yc = y - mu
        var = jnp.mean(yc * yc, axis=-1, keepdims=True)
        yn = yc * lax.rsqrt(var + EPS) * gain_ref[...]
        z_ref[span(i), :] = (yn * s_ref[span(i), :].astype(F32)).astype(BF16)

    return ([functools.partial(kv_stage, i) for i in range(n_chunks)] + [state_stage]
            + [functools.partial(out_stage, i) for i in range(n_chunks)])


def _att_stages(tile, n_tiles, sink_ref, qt_ref, k_ref, kp_ref, kn_ref, vt_ref, vtp_ref, vtn_ref, ot_ref):
    blk = ATT_BLOCK
    n_blocks = qt_ref.shape[1] // blk
    grp = ATT_Q_HEADS // ATT_KV_HEADS
    hd = ATT_HD
    neg_inf = jnp.float32(-jnp.inf)
    kpos = lax.broadcasted_iota(jnp.int32, (blk, 2 * blk), 0)
    qpos = lax.broadcasted_iota(jnp.int32, (blk, 2 * blk), 1) % blk
    prev_bias = jnp.where(kpos >= qpos, 0.0, neg_inf)
    next_bias = jnp.where(kpos <= qpos, 0.0, neg_inf)
    first_prev_bias = prev_bias + jnp.where(tile > 0, 0.0, neg_inf)
    last_next_bias = next_bias + jnp.where(tile < n_tiles - 1, 0.0, neg_inf)
    left = lax.broadcasted_iota(jnp.int32, (1, 2 * blk), 1) < blk

    def cols(j):
        return slice(j * blk, (j + 1) * blk)

    def scores(j, g, p):
        qt = jnp.concatenate([qt_ref[cols(2 * g), cols(j)], qt_ref[cols(2 * g + 1), cols(j)]], axis=1)
        kv = cols(2 * g + p)
        kwin = jnp.concatenate([kp_ref[:, kv] if j == 0 else k_ref[cols(j - 1), kv], k_ref[cols(j), kv],
                                kn_ref[:, kv] if j == n_blocks - 1 else k_ref[cols(j + 1), kv]], axis=0)
        return _dot(kwin, qt)

    def finish(j, g, p, s):
        s_prev = s[cols(0), :] + (first_prev_bias if j == 0 else prev_bias)
        s_cur = s[cols(1), :]
        s_next = s[cols(2), :] + (last_next_bias if j == n_blocks - 1 else next_bias)
        snk = jnp.where(left, sink_ref[grp * g + p], sink_ref[grp * g + 2 + p]) * LOG2_E
        m = jnp.maximum(jnp.max(jnp.maximum(jnp.maximum(s_prev, s_cur), s_next), axis=0, keepdims=True), snk)
        et = jnp.concatenate([jnp.exp2(s_prev - m).astype(BF16), jnp.exp2(s_cur - m).astype(BF16),
                              jnp.exp2(s_next - m).astype(BF16)], axis=0)
        dims = slice(g * hd, (g + 1) * hd)
        vt = jnp.concatenate([vtp_ref[dims, :] if j == 0 else vt_ref[dims, cols(j - 1)],
                              vt_ref[dims, cols(j)],
                              vtn_ref[dims, :] if j == n_blocks - 1 else vt_ref[dims, cols(j + 1)]], axis=1)
        acc = _dot(jnp.concatenate([vt, jnp.ones((ONES_ROWS, 3 * blk), BF16)], axis=0), et)
        denom = acc[hd:hd + 1, :] + jnp.exp2(snk - m)
        ot = acc[:hd, :] * (1.0 / denom)
        head_a = grp * g + p
        head_b = grp * g + 2 + p
        ot_ref[head_a * hd:(head_a + 1) * hd, cols(j)] = ot[:, :blk].astype(BF16)
        ot_ref[head_b * hd:(head_b + 1) * hd, cols(j)] = ot[:, blk:].astype(BF16)

    units = [(j, g, p) for j in range(n_blocks) for g in range(ATT_KV_HEADS) for p in range(2)]
    pending = []

    def stage(n):
        if n == 0:
            pending.extend(scores(*u) for u in units[:ATT_SCORES_AHEAD])
        if n + ATT_SCORES_AHEAD < len(units):
            pending.append(scores(*units[n + ATT_SCORES_AHEAD]))
        finish(*units[n], pending.pop(0))

    return [functools.partial(stage, n) for n in range(len(units))]


def _interleave(a, b):
    if len(a) < len(b):
        a, b = b, a
    out, taken = [], 0
    for n, stage in enumerate(a):
        out.append(stage)
        due = (n + 1) * len(b) // len(a)
        out += b[taken:due]
        taken = due
    return out


def _mixer_kernel(dec_ref, sink_ref, q_ref, kt_ref, v_ref, s_ref, gain_ref,
                  qt_ref, k_ref, kp_ref, kn_ref, vt_ref, vtp_ref, vtn_ref,
                  z_ref, ot_ref, kv_ref, st_ref):
    j = pl.program_id(1)
    ret = _ret_stages(j, dec_ref, q_ref, kt_ref, v_ref, s_ref, gain_ref, z_ref, kv_ref, st_ref)
    att = _att_stages(j, pl.num_programs(1), sink_ref, qt_ref, k_ref, kp_ref, kn_ref,
                      vt_ref, vtp_ref, vtn_ref, ot_ref)
    for stage in _interleave(ret, att):
        stage()


def _mixer_call(dec, sink, qr, krt, vr, sr, gain, qat, k4, vat, batch, seq):
    t, ret_w = qr.shape
    att_w = qat.shape[0]
    tq = ATT_TILE
    n_tiles = seq // tq
    assert n_tiles == RET_HEADS, "one attention tile per retention head in each grid step"
    per_tile = tq // ATT_BLOCK
    n_halo = seq // ATT_BLOCK
    n_chunks = seq // RET_TILE

    def prev_blk(b, i):
        return b * n_halo + jnp.maximum(i * per_tile - 1, 0)

    def next_blk(b, i):
        return b * n_halo + jnp.minimum((i + 1) * per_tile, n_halo - 1)

    def dims_by_tokens(rows):
        return (pl.BlockSpec((rows, tq), lambda b, i: (0, b * n_tiles + i)),
                pl.BlockSpec((rows, ATT_BLOCK), lambda b, i: (0, prev_blk(b, i))),
                pl.BlockSpec((rows, ATT_BLOCK), lambda b, i: (0, next_blk(b, i))))

    kw = k4.shape[1]
    smem = pl.BlockSpec(memory_space=pltpu.SMEM)
    head = pl.BlockSpec((seq, RET_DK), lambda b, h: (b, h))
    return pl.pallas_call(
        _mixer_kernel,
        grid=(batch, n_tiles),
        in_specs=[smem, smem, head, pl.BlockSpec((RET_DK, seq), lambda b, h: (h, b)), head, head,
                  pl.BlockSpec((1, RET_DK), lambda b, h: (0, h)),
                  dims_by_tokens(att_w)[0],
                  pl.BlockSpec((tq, kw), lambda b, i: (b * n_tiles + i, 0)),
                  pl.BlockSpec((ATT_BLOCK, kw), lambda b, i: (prev_blk(b, i), 0)),
                  pl.BlockSpec((ATT_BLOCK, kw), lambda b, i: (next_blk(b, i), 0)),
                  *dims_by_tokens(vat.shape[0])],
        out_specs=[head, dims_by_tokens(att_w)[0]],
        out_shape=[jax.ShapeDtypeStruct((t, ret_w), BF16), jax.ShapeDtypeStruct((att_w, t), BF16)],
        scratch_shapes=[pltpu.VMEM((n_chunks, RET_DK, 2 * RET_DK), F32),
                        pltpu.VMEM((n_chunks, RET_DK, 2 * RET_DK), BF16)],
        compiler_params=_cparams(2),
        name="mixers",
    )(dec, sink, qr, krt, vr, sr, gain, qat, k4, k4, k4, vat, vat, vat)


def _tail_kernel(x_ref, zr_ref, oa_ref, sgr_ref, sga_ref, wro_ref, wao_ref, wout_ref,
                 gn_ref, wg_ref, wu_ref, wo_ref, gfin_ref, y_ref, h_ref, a_ref):
    y_ret = _dot(zr_ref[...], wro_ref[...])
    y_att = lax.dot_general(oa_ref[...], wao_ref[...], (((0,), (0,)), ((), ())), preferred_element_type=F32)
    merged = sgr_ref[...].astype(F32) * y_ret + sga_ref[...].astype(F32) * y_att
    x2 = x_ref[...] + _dot(merged.astype(BF16), wout_ref[...])
    group = h_ref.shape[0]
    for r in range(x2.shape[0] // group):
        rows = slice(r * group, (r + 1) * group)
        x3 = _swiglu_residual(x2[rows, :], gn_ref, wg_ref, wu_ref, wo_ref, h_ref, a_ref)
        y_ref[rows, :] = _rms(x3, gfin_ref[...])


def _tail_call(x1, zr, oa, sgr, sga, wro, wao, wout, gn, wg, wu, wo, gfin):
    t, d = x1.shape
    tm = TOKEN_TILE
    ffp = wo.shape[0]
    tok = lambda width: pl.BlockSpec((tm, width), lambda i: (i, 0))
    return pl.pallas_call(
        _tail_kernel,
        grid=(t // tm,),
        in_specs=[tok(d), tok(zr.shape[1]), pl.BlockSpec((oa.shape[0], tm), lambda i: (0, i)), tok(d), tok(d),
                  _resident(wro.shape), _resident(wao.shape), _resident(wout.shape),
                  _resident(gn.shape), _resident(wg.shape), _resident(wu.shape), _resident(wo.shape),
                  _resident(gfin.shape)],
        out_specs=tok(d),
        out_shape=jax.ShapeDtypeStruct((t, d), F32),
        scratch_shapes=[pltpu.VMEM((TAIL_FFN_ROWS, d), BF16), pltpu.VMEM((TAIL_FFN_ROWS, ffp), BF16)],
        compiler_params=_cparams(1),
        name="tail",
    )(x1, zr, oa, sgr, sga, wro, wao, wout, gn, wg, wu, wo, gfin)


def _prep_ffn(w_in, w_out):
    d, two_ff = w_in.shape
    ff = two_ff // 2
    ffp = -(-ff // FF_CHUNK) * FF_CHUNK
    pad = ((0, 0), (0, ffp - ff))
    wg = jnp.pad(w_in[:, :ff].astype(BF16), pad)
    wu = jnp.pad(w_in[:, ff:].astype(BF16), pad)
    wo = jnp.pad(w_out.astype(BF16), ((0, ffp - ff), (0, 0)))
    return wg, wu, wo


def _rope_tables(seq):
    pos = np.arange(seq, dtype=np.float32)[:, None]

    def cs(half):
        inv_freq = np.float32(ROPE_THETA) ** (-np.arange(half, dtype=np.float32) / np.float32(half))
        ang = (pos * inv_freq[None, :].astype(np.float32)).astype(np.float64)
        return np.cos(ang).astype(np.float32), np.sin(ang).astype(np.float32)

    c, s = cs(RET_DK // 2)
    cosr = np.concatenate([c, c], axis=1)
    sinr = np.concatenate([-s, s], axis=1)
    c, s = cs(ATT_HD // 2)
    cosa = np.concatenate([c, c, c, c], axis=1)
    sina = np.concatenate([-s, s, -s, s], axis=1)
    return tuple(jnp.asarray(tab) for tab in (cosr, sinr, cosa, sina))


def kernel(x, norm_ffn1, ffn1_w_in, ffn1_w_out, norm_mix, w_in, ret_decay_fwd, ret_decay_bwd,
           ret_gn_gain, w_ret_out, att_sink, w_att_out, w_out, norm_ffn2, ffn2_w_in, ffn2_w_out,
           norm_final):
    b, s, d = x.shape
    assert ffn1_w_in.shape[0] == 1, "single-layer block"
    assert all(s % tile == 0 for tile in (TOKEN_TILE, FFN_TILE, PROJ_TILE, RET_TILE, ATT_TILE))
    xt = x.reshape(b * s, d)
    ffn1_w = _prep_ffn(ffn1_w_in[0], ffn1_w_out[0])
    ffn2_w = _prep_ffn(ffn2_w_in[0], ffn2_w_out[0])
    cosr, sinr, cosa, sina = _rope_tables(s)

    x1 = _ffn_call(xt, norm_ffn1, *ffn1_w)
    qr, krt, vr, sr, qat, k4, vat, sgr, sga = _proj_call(
        x1, norm_mix, w_in[0].astype(BF16), cosr, sinr, cosa, sina, s)
    dec = jnp.concatenate([ret_decay_fwd, ret_decay_bwd], axis=0)
    zr, oat = _mixer_call(dec, att_sink[0], qr, krt, vr, sr, ret_gn_gain, qat, k4, vat, b, s)
    y = _tail_call(x1, zr, oat, sgr, sga, w_ret_out[0].astype(BF16), w_att_out[0].astype(BF16),
                   w_out[0].astype(BF16), norm_ffn2, *ffn2_w, norm_final.reshape(1, d))
    return y.reshape(b, s, d)
```

```python
import functools

import jax
import jax.numpy as jnp
import numpy as np
from jax import lax
from jax.experimental import pallas as pl
from jax.experimental.pallas import tpu as pltpu

F32 = jnp.float32
BF16 = jnp.bfloat16

EPS = 1e-6
LOG2_E = 1.4426950408889634
ROPE_THETA = 10000.0
RET_HEADS = 4
RET_DK = 128
ATT_Q_HEADS = 8
ATT_KV_HEADS = 2
ATT_HD = 64
ATT_BLOCK = 128

LANES = 128
BF16_SUBLANES = 16
MXU_WIDTH = 256
V7X_VMEM_BYTES = 64 * 1024 * 1024
VMEM_LIMIT = V7X_VMEM_BYTES - 6 * 1024 * 1024

FF_CHUNK = MXU_WIDTH
RET_TILE = 256
ATT_TILE = 1024
ATT_SCORES_AHEAD = 3
FFN_TILE = 1024
PROJ_TILE = 1024
PROJ_RUN_ORDER = (6, 0, 7, 1, 3, 4, 5, 2)
TAIL_TILE = 1024
TAIL_FFN_ROWS = 512


def _cparams(n_grid):
    return pltpu.CompilerParams(dimension_semantics=("arbitrary",) * n_grid,
                                vmem_limit_bytes=VMEM_LIMIT)


def _resident(shape):
    zeros = (0,) * len(shape)
    return pl.BlockSpec(shape, lambda *_: zeros, pipeline_mode=pl.Buffered(1))


def _rms(x, g):
    y = x * lax.rsqrt(jnp.mean(x * x, axis=-1, keepdims=True) + EPS)
    return y * g


def _dot(a, b):
    return jnp.dot(a, b, preferred_element_type=F32)


def _swiglu_residual(x, gn_ref, wg_ref, wu_ref, wo_ref, h_ref, a_ref):
    h_ref[...] = _rms(x, gn_ref[...]).astype(BF16)
    fc = FF_CHUNK
    for c in range(wg_ref.shape[1] // fc):
        chunk = slice(c * fc, (c + 1) * fc)
        gu = _dot(h_ref[...], jnp.concatenate([wg_ref[:, chunk], wu_ref[:, chunk]], axis=1))
        a_ref[:, chunk] = (jax.nn.silu(gu[:, :fc]) * gu[:, fc:]).astype(BF16)
    return x + 0.5 * _dot(a_ref[...], wo_ref[...])


def _ffn_kernel(x_ref, gn_ref, wg_ref, wu_ref, wo_ref, o_ref, h_ref, a_ref):
    o_ref[...] = _swiglu_residual(x_ref[...], gn_ref, wg_ref, wu_ref, wo_ref, h_ref, a_ref)


def _ffn_call(x, gn, wg, wu, wo):
    t, d = x.shape
    tm = FFN_TILE
    ffp = wo.shape[0]
    tok = pl.BlockSpec((tm, d), lambda i: (i, 0))
    return pl.pallas_call(
        _ffn_kernel,
        grid=(t // tm,),
        in_specs=[tok, _resident(gn.shape), _resident(wg.shape), _resident(wu.shape), _resident(wo.shape)],
        out_specs=tok,
        out_shape=jax.ShapeDtypeStruct((t, d), F32),
        scratch_shapes=[pltpu.VMEM((tm, d), BF16), pltpu.VMEM((tm, ffp), BF16)],
        compiler_params=_cparams(1),
        name="ffn1",
    )(x, gn, wg, wu, wo)


def _rope128(y, cos, sin_signed):
    return y * cos + pltpu.roll(y, 64, 1) * sin_signed


def _rope64(y, cos, sin_signed):
    first_half = (lax.broadcasted_iota(jnp.int32, y.shape, 1) // (ATT_HD // 2)) % 2 == 0
    partner = jnp.where(first_half, pltpu.roll(y, LANES - ATT_HD // 2, 1), pltpu.roll(y, ATT_HD // 2, 1))
    return y * cos + partner * sin_signed


def _proj_kernel(x_ref, gn_ref, w_ref, cosr_ref, sinr_ref, cosa_ref, sina_ref,
                 qr_ref, kr_ref, vr_ref, sr_ref, qa_ref, k4_ref, va_ref, sgr_ref, sga_ref,
                 h_ref, *, ret_w, att_qw):
    h_ref[...] = _rms(x_ref[...], gn_ref[...]).astype(BF16)

    def proj(a, b):
        return _dot(h_ref[...], w_ref[:, a:b])

    cosr, sinr = cosr_ref[...], sinr_ref[...]
    cosa, sina = cosa_ref[...], sina_ref[...]
    d = x_ref.shape[1]
    lane_blocks = [slice(j * LANES, (j + 1) * LANES) for j in range(ret_w // LANES)]
    assert att_qw == ret_w

    def ret_q(c0):
        y = proj(c0, c0 + ret_w)
        for sl in lane_blocks:
            qr_ref[:, sl] = _rope128(y[:, sl], cosr, sinr).astype(BF16)

    def ret_k(c0):
        y = proj(c0, c0 + ret_w)
        for sl in lane_blocks:
            kr_ref[sl, :] = (_rope128(y[:, sl], cosr, sinr) * (RET_DK ** -0.5)).T.astype(BF16)

    def ret_v(c0):
        vr_ref[...] = proj(c0, c0 + ret_w).astype(BF16)

    def ret_g(c0):
        sr_ref[...] = jax.nn.silu(proj(c0, c0 + ret_w)).astype(BF16)

    def att_q(c0):
        y = proj(c0, c0 + att_qw)
        for sl in lane_blocks:
            qa_ref[sl, :] = (_rope64(y[:, sl], cosa, sina) * (ATT_HD ** -0.5 * LOG2_E)).T.astype(BF16)

    def att_kv(c0):
        y = proj(c0, c0 + 2 * LANES)
        kk = _rope64(y[:, :LANES], cosa, sina)
        low = lax.broadcasted_iota(jnp.int32, kk.shape, 1) < ATT_HD
        kswap = pltpu.roll(kk, ATT_HD, 1)
        k4_ref[:, 0 * LANES:1 * LANES] = jnp.where(low, kk, 0.0).astype(BF16)
        k4_ref[:, 1 * LANES:2 * LANES] = jnp.where(low, 0.0, kswap).astype(BF16)
        k4_ref[:, 2 * LANES:3 * LANES] = jnp.where(low, kswap, 0.0).astype(BF16)
        k4_ref[:, 3 * LANES:4 * LANES] = jnp.where(low, 0.0, kk).astype(BF16)
        va_ref[...] = y[:, LANES:].T.astype(BF16)

    def gate_r(c0):
        sgr_ref[...] = jax.nn.sigmoid(proj(c0, c0 + d)).astype(BF16)

    def gate_a(c0):
        sga_ref[...] = jax.nn.sigmoid(proj(c0, c0 + d)).astype(BF16)

    groups = [(ret_q, ret_w), (ret_k, ret_w), (ret_v, ret_w), (ret_g, ret_w), (att_q, att_qw),
              (att_kv, 2 * LANES), (gate_r, d), (gate_a, d)]
    starts = np.cumsum([0] + [width for _, width in groups])
    for n in PROJ_RUN_ORDER:
        groups[n][0](int(starts[n]))


def _proj_call(x1, gn, w, cosr, sinr, cosa, sina, seq):
    t, d = x1.shape
    tm = PROJ_TILE
    ret_w = RET_HEADS * RET_DK
    att_qw = ATT_Q_HEADS * ATT_HD
    n_pos = seq // tm
    tok = lambda width: pl.BlockSpec((tm, width), lambda i: (i, 0))
    tab = pl.BlockSpec((tm, LANES), lambda i: (i % n_pos, 0))
    widths = (ret_w, ret_w, ret_w, ret_w, att_qw, 4 * LANES, LANES, d, d)
    out_specs = [tok(wd) for wd in widths]
    out_shape = [jax.ShapeDtypeStruct((t, wd), BF16) for wd in widths]
    for k_out in (1, 4, 6):
        out_specs[k_out] = pl.BlockSpec((widths[k_out], tm), lambda i: (0, i))
        out_shape[k_out] = jax.ShapeDtypeStruct((widths[k_out], t), BF16)
    return pl.pallas_call(
        functools.partial(_proj_kernel, ret_w=ret_w, att_qw=att_qw),
        grid=(t // tm,),
        in_specs=[tok(d), _resident(gn.shape), _resident(w.shape), tab, tab, tab, tab],
        out_specs=out_specs,
        out_shape=out_shape,
        scratch_shapes=[pltpu.VMEM((tm, d), BF16)],
        compiler_params=_cparams(1),
        name="proj",
    )(x1, gn, w, cosr, sinr, cosa, sina)


def _ret_stages(h, dec_ref, q_ref, kt_ref, v_ref, s_ref, gain_ref, z_ref, kv_ref, st_ref):
    c = RET_TILE
    dk = kt_ref.shape[0]
    n_chunks = q_ref.shape[0] // c

    def log_gamma(direction, shape):
        return -jnp.exp(jnp.full(shape, dec_ref[direction, h], F32))

    def iota(shape, axis):
        return lax.broadcasted_iota(jnp.int32, shape, axis).astype(F32)

    rel = iota((c, c), 0) - iota((c, c), 1)
    decay = jnp.where(rel >= 0, jnp.exp(jnp.maximum(rel, 0.0) * log_gamma(0, (c, c))),
                      jnp.exp(jnp.maximum(-rel, 0.0) * log_gamma(1, (c, c))))
    pos = iota((c, dk), 0)
    kdec_f = jnp.exp((c - 1 - pos) * log_gamma(0, (c, dk)))
    kdec_b = jnp.exp(pos * log_gamma(1, (c, dk)))
    qdec_f = jnp.exp((pos + 1.0) * log_gamma(0, (c, dk)))
    qdec_b = jnp.exp((c - pos) * log_gamma(1, (c, dk)))
    g_f = jnp.exp(c * log_gamma(0, (dk, dk)))
    g_b = jnp.exp(c * log_gamma(1, (dk, dk)))

    def span(i):
        return slice(i * c, (i + 1) * c)

    def kv_stage(i):
        v = v_ref[span(i), :].astype(F32)
        rhs = jnp.concatenate([(v * kdec_f).astype(BF16), (v * kdec_b).astype(BF16)], axis=1)
        kv_ref[i] = _dot(kt_ref[:, span(i)], rhs)

    def state_stage():
        state = jnp.zeros((dk, dk), F32)
        for i in range(n_chunks):
            st_ref[i, :, :dk] = state.astype(BF16)
            state = g_f * state + kv_ref[i, :, :dk]
        state = jnp.zeros((dk, dk), F32)
        for i in reversed(range(n_chunks)):
            st_ref[i, :, dk:] = state.astype(BF16)
            state = g_b * state + kv_ref[i, :, dk:]

    def out_stage(i):
        q = q_ref[span(i), :]
        scores = _dot(q, kt_ref[:, span(i)])
        inner = _dot((scores * decay).astype(BF16), v_ref[span(i), :])
        cross = _dot(q, st_ref[i])
        y = inner + qdec_f * cross[:, :dk] + qdec_b * cross[:, dk:]
        mu = jnp.mean(y, axis=-1, keepdims=True)
        yc = y - mu
        var = jnp.mean(yc * yc, axis=-1, keepdims=True)
        yn = yc * lax.rsqrt(var + EPS) * gain_ref[...]
        z_ref[span(i), :] = (yn * s_ref[span(i), :].astype(F32)).astype(BF16)

    return ([functools.partial(kv_stage, i) for i in range(n_chunks)] + [state_stage]
            + [functools.partial(out_stage, i) for i in range(n_chunks)])


def _att_stages(tile, n_tiles, sink_ref, qt_ref, k_ref, kp_ref, kn_ref, vt_ref, vtp_ref, vtn_ref, ot_ref):
    blk = ATT_BLOCK
    n_blocks = qt_ref.shape[1] // blk
    grp = ATT_Q_HEADS // ATT_KV_HEADS
    hd = ATT_HD
    neg_inf = jnp.float32(-jnp.inf)
    kpos = lax.broadcasted_iota(jnp.int32, (blk, 2 * blk), 0)
    qpos = lax.broadcasted_iota(jnp.int32, (blk, 2 * blk), 1) % blk
    prev_bias = jnp.where(kpos >= qpos, 0.0, neg_inf)
    next_bias = jnp.where(kpos <= qpos, 0.0, neg_inf)
    first_prev_bias = prev_bias + jnp.where(tile > 0, 0.0, neg_inf)
    last_next_bias = next_bias + jnp.where(tile < n_tiles - 1, 0.0, neg_inf)
    left = lax.broadcasted_iota(jnp.int32, (1, 2 * blk), 1) < blk

    def cols(j):
        return slice(j * blk, (j + 1) * blk)

    def scores(j, g, p):
        qt = jnp.concatenate([qt_ref[cols(2 * g), cols(j)], qt_ref[cols(2 * g + 1), cols(j)]], axis=1)
        kv = cols(2 * g + p)
        kwin = jnp.concatenate([kp_ref[:, kv] if j == 0 else k_ref[cols(j - 1), kv], k_ref[cols(j), kv],
                                kn_ref[:, kv] if j == n_blocks - 1 else k_ref[cols(j + 1), kv]], axis=0)
        return _dot(kwin, qt)

    def finish(j, g, p, s):
        s_prev = s[cols(0), :] + (first_prev_bias if j == 0 else prev_bias)
        s_cur = s[cols(1), :]
        s_next = s[cols(2), :] + (last_next_bias if j == n_blocks - 1 else next_bias)
        snk = jnp.where(left, sink_ref[grp * g + p], sink_ref[grp * g + 2 + p]) * LOG2_E
        m = jnp.maximum(jnp.max(jnp.maximum(jnp.maximum(s_prev, s_cur), s_next), axis=0, keepdims=True), snk)
        et = jnp.concatenate([jnp.exp2(s_prev - m).astype(BF16), jnp.exp2(s_cur - m).astype(BF16),
                              jnp.exp2(s_next - m).astype(BF16)], axis=0)
        dims = slice(g * hd, (g + 1) * hd)
        vt = jnp.concatenate([vtp_ref[dims, :] if j == 0 else vt_ref[dims, cols(j - 1)],
                              vt_ref[dims, cols(j)],
                              vtn_ref[dims, :] if j == n_blocks - 1 else vt_ref[dims, cols(j + 1)]], axis=1)
        acc = _dot(jnp.concatenate([vt, jnp.ones((BF16_SUBLANES, 3 * blk), BF16)], axis=0), et)
        denom = acc[hd:hd + 1, :] + jnp.exp2(snk - m)
        ot = acc[:hd, :] * (1.0 / denom)
        head_a = grp * g + p
        head_b = grp * g + 2 + p
        ot_ref[head_a * hd:(head_a + 1) * hd, cols(j)] = ot[:, :blk].astype(BF16)
        ot_ref[head_b * hd:(head_b + 1) * hd, cols(j)] = ot[:, blk:].astype(BF16)

    units = [(j, g, p) for j in range(n_blocks) for g in range(ATT_KV_HEADS) for p in range(2)]
    pending = []

    def stage(n):
        if n == 0:
            pending.extend(scores(*u) for u in units[:ATT_SCORES_AHEAD])
        if n + ATT_SCORES_AHEAD < len(units):
            pending.append(scores(*units[n + ATT_SCORES_AHEAD]))
        finish(*units[n], pending.pop(0))

    return [functools.partial(stage, n) for n in range(len(units))]


def _interleave(a, b):
    if len(a) < len(b):
        a, b = b, a
    out, taken = [], 0
    for n, stage in enumerate(a):
        out.append(stage)
        due = (n + 1) * len(b) // len(a)
        out += b[taken:due]
        taken = due
    return out


def _mixer_kernel(dec_ref, sink_ref, q_ref, kt_ref, v_ref, s_ref, gain_ref,
                  qt_ref, k_ref, kp_ref, kn_ref, vt_ref, vtp_ref, vtn_ref,
                  z_ref, ot_ref, kv_ref, st_ref):
    j = pl.program_id(1)
    ret = _ret_stages(j, dec_ref, q_ref, kt_ref, v_ref, s_ref, gain_ref, z_ref, kv_ref, st_ref)
    att = _att_stages(j, pl.num_programs(1), sink_ref, qt_ref, k_ref, kp_ref, kn_ref,
                      vt_ref, vtp_ref, vtn_ref, ot_ref)
    for stage in _interleave(ret, att):
        stage()


def _mixer_call(dec, sink, qr, krt, vr, sr, gain, qat, k4, vat, batch, seq):
    t, ret_w = qr.shape
    att_w = qat.shape[0]
    tq = ATT_TILE
    n_tiles = seq // tq
    assert n_tiles == RET_HEADS, "one attention tile per retention head in each grid step"
    per_tile = tq // ATT_BLOCK
    n_halo = seq // ATT_BLOCK
    n_chunks = seq // RET_TILE

    def prev_blk(b, i):
        return b * n_halo + jnp.maximum(i * per_tile - 1, 0)

    def next_blk(b, i):
        return b * n_halo + jnp.minimum((i + 1) * per_tile, n_halo - 1)

    def dims_by_tokens(rows):
        return (pl.BlockSpec((rows, tq), lambda b, i: (0, b * n_tiles + i)),
                pl.BlockSpec((rows, ATT_BLOCK), lambda b, i: (0, prev_blk(b, i))),
                pl.BlockSpec((rows, ATT_BLOCK), lambda b, i: (0, next_blk(b, i))))

    kw = k4.shape[1]
    smem = pl.BlockSpec(memory_space=pltpu.SMEM)
    head = pl.BlockSpec((seq, RET_DK), lambda b, h: (b, h))
    return pl.pallas_call(
        _mixer_kernel,
        grid=(batch, n_tiles),
        in_specs=[smem, smem, head, pl.BlockSpec((RET_DK, seq), lambda b, h: (h, b)), head, head,
                  pl.BlockSpec((1, RET_DK), lambda b, h: (0, h)),
                  dims_by_tokens(att_w)[0],
                  pl.BlockSpec((tq, kw), lambda b, i: (b * n_tiles + i, 0)),
                  pl.BlockSpec((ATT_BLOCK, kw), lambda b, i: (prev_blk(b, i), 0)),
                  pl.BlockSpec((ATT_BLOCK, kw), lambda b, i: (next_blk(b, i), 0)),
                  *dims_by_tokens(vat.shape[0])],
        out_specs=[head, dims_by_tokens(att_w)[0]],
        out_shape=[jax.ShapeDtypeStruct((t, ret_w), BF16), jax.ShapeDtypeStruct((att_w, t), BF16)],
        scratch_shapes=[pltpu.VMEM((n_chunks, RET_DK, 2 * RET_DK), F32),
                        pltpu.VMEM((n_chunks, RET_DK, 2 * RET_DK), BF16)],
        compiler_params=_cparams(2),
        name="mixers",
    )(dec, sink, qr, krt, vr, sr, gain, qat, k4, k4, k4, vat, vat, vat)


def _tail_kernel(x_ref, zr_ref, oa_ref, sgr_ref, sga_ref, wro_ref, wao_ref, wout_ref,
                 gn_ref, wg_ref, wu_ref, wo_ref, gfin_ref, y_ref, h_ref, a_ref):
    y_ret = _dot(zr_ref[...], wro_ref[...])
    y_att = lax.dot_general(oa_ref[...], wao_ref[...], (((0,), (0,)), ((), ())), preferred_element_type=F32)
    merged = sgr_ref[...].astype(F32) * y_ret + sga_ref[...].astype(F32) * y_att
    x2 = x_ref[...] + _dot(merged.astype(BF16), wout_ref[...])
    group = h_ref.shape[0]
    for r in range(x2.shape[0] // group):
        rows = slice(r * group, (r + 1) * group)
        x3 = _swiglu_residual(x2[rows, :], gn_ref, wg_ref, wu_ref, wo_ref, h_ref, a_ref)
        y_ref[rows, :] = _rms(x3, gfin_ref[...])


def _tail_call(x1, zr, oa, sgr, sga, wro, wao, wout, gn, wg, wu, wo, gfin):
    t, d = x1.shape
    tm = TAIL_TILE
    ffp = wo.shape[0]
    tok = lambda width: pl.BlockSpec((tm, width), lambda i: (i, 0))
    return pl.pallas_call(
        _tail_kernel,
        grid=(t // tm,),
        in_specs=[tok(d), tok(zr.shape[1]), pl.BlockSpec((oa.shape[0], tm), lambda i: (0, i)), tok(d), tok(d),
                  _resident(wro.shape), _resident(wao.shape), _resident(wout.shape),
                  _resident(gn.shape), _resident(wg.shape), _resident(wu.shape), _resident(wo.shape),
                  _resident(gfin.shape)],
        out_specs=tok(d),
        out_shape=jax.ShapeDtypeStruct((t, d), F32),
        scratch_shapes=[pltpu.VMEM((TAIL_FFN_ROWS, d), BF16), pltpu.VMEM((TAIL_FFN_ROWS, ffp), BF16)],
        compiler_params=_cparams(1),
        name="tail",
    )(x1, zr, oa, sgr, sga, wro, wao, wout, gn, wg, wu, wo, gfin)


def _prep_ffn(w_in, w_out):
    d, two_ff = w_in.shape
    ff = two_ff // 2
    ffp = -(-ff // FF_CHUNK) * FF_CHUNK
    pad = ((0, 0), (0, ffp - ff))
    wg = jnp.pad(w_in[:, :ff].astype(BF16), pad)
    wu = jnp.pad(w_in[:, ff:].astype(BF16), pad)
    wo = jnp.pad(w_out.astype(BF16), ((0, ffp - ff), (0, 0)))
    return wg, wu, wo


def _rope_tables(seq):
    pos = np.arange(seq, dtype=np.float32)[:, None]

    def cs(half):
        inv_freq = np.float32(ROPE_THETA) ** (-np.arange(half, dtype=np.float32) / np.float32(half))
        ang = (pos * inv_freq[None, :].astype(np.float32)).astype(np.float64)
        return np.cos(ang).astype(np.float32), np.sin(ang).astype(np.float32)

    c, s = cs(RET_DK // 2)
    cosr = np.concatenate([c, c], axis=1)
    sinr = np.concatenate([-s, s], axis=1)
    c, s = cs(ATT_HD // 2)
    cosa = np.concatenate([c, c, c, c], axis=1)
    sina = np.concatenate([-s, s, -s, s], axis=1)
    return tuple(jnp.asarray(tab) for tab in (cosr, sinr, cosa, sina))


def kernel(x, norm_ffn1, ffn1_w_in, ffn1_w_out, norm_mix, w_in, ret_decay_fwd, ret_decay_bwd,
           ret_gn_gain, w_ret_out, att_sink, w_att_out, w_out, norm_ffn2, ffn2_w_in, ffn2_w_out,
           norm_final):
    b, s, d = x.shape
    assert ffn1_w_in.shape[0] == 1, "single-layer block"
    assert all(s % tile == 0 for tile in (TAIL_TILE, FFN_TILE, PROJ_TILE, RET_TILE, ATT_TILE))
    assert TAIL_TILE % TAIL_FFN_ROWS == 0
    xt = x.reshape(b * s, d)
    ffn1_w = _prep_ffn(ffn1_w_in[0], ffn1_w_out[0])
    ffn2_w = _prep_ffn(ffn2_w_in[0], ffn2_w_out[0])
    cosr, sinr, cosa, sina = _rope_tables(s)

    x1 = _ffn_call(xt, norm_ffn1, *ffn1_w)
    qr, krt, vr, sr, qat, k4, vat, sgr, sga = _proj_call(
        x1, norm_mix, w_in[0].astype(BF16), cosr, sinr, cosa, sina, s)
    dec = jnp.concatenate([ret_decay_fwd, ret_decay_bwd], axis=0)
    zr, oat = _mixer_call(dec, att_sink[0], qr, krt, vr, sr, ret_gn_gain, qat, k4, vat, b, s)
    y = _tail_call(x1, zr, oat, sgr, sga, w_ret_out[0].astype(BF16), w_att_out[0].astype(BF16),
                   w_out[0].astype(BF16), norm_ffn2, *ffn2_w, norm_final.reshape(1, d))
    return y.reshape(b, s, d)
```

```python
import functools

import jax
import jax.numpy as jnp
import numpy as np
from jax import lax
from jax.experimental import pallas as pl
from jax.experimental.pallas import tpu as pltpu

F32 = jnp.float32
BF16 = jnp.bfloat16

EPS = 1e-6
LOG2_E = 1.4426950408889634
ROPE_THETA = 10000.0
RET_HEADS = 4
RET_DK = 128
ATT_Q_HEADS = 8
ATT_KV_HEADS = 2
ATT_HD = 64
ATT_BLOCK = 128

LANES = 128
BF16_SUBLANES = 16
MXU_WIDTH = 256
V7X_VMEM_BYTES = 64 * 1024 * 1024
VMEM_LIMIT = V7X_VMEM_BYTES - 6 * 1024 * 1024

FF_CHUNK = MXU_WIDTH
RET_TILE = 256
ATT_TILE = 1024
ATT_SCORES_AHEAD = 3
FFN_TILE = 1024
STAGE_IN_ROWS = 64
STAGE_OUT_ROWS = 256
PROJ_TILE = 1024
PROJ_RUN_ORDER = (6, 0, 7, 1, 3, 4, 5, 2)
TAIL_TILE = 1024
TAIL_FFN_ROWS = 512


def _cparams(n_grid):
    return pltpu.CompilerParams(dimension_semantics=("arbitrary",) * n_grid,
                                vmem_limit_bytes=VMEM_LIMIT)


def _resident(shape):
    zeros = (0,) * len(shape)
    return pl.BlockSpec(shape, lambda *_: zeros, pipeline_mode=pl.Buffered(1))


def _rms(x, g):
    y = x * lax.rsqrt(jnp.mean(x * x, axis=-1, keepdims=True) + EPS)
    return y * g


def _dot(a, b):
    return jnp.dot(a, b, preferred_element_type=F32)


def _swiglu_residual(x, gn_ref, wg_ref, wu_ref, wo_ref, h_ref, a_ref):
    h_ref[...] = _rms(x, gn_ref[...]).astype(BF16)
    fc = FF_CHUNK
    for c in range(wg_ref.shape[1] // fc):
        chunk = slice(c * fc, (c + 1) * fc)
        gu = _dot(h_ref[...], jnp.concatenate([wg_ref[:, chunk], wu_ref[:, chunk]], axis=1))
        a_ref[:, chunk] = (jax.nn.silu(gu[:, :fc]) * gu[:, fc:]).astype(BF16)
    return x + 0.5 * _dot(a_ref[...], wo_ref[...])


def _stream_rows(w_hbm, stage_ref, sem_ref, slab, consume):
    n_rows = w_hbm.shape[0]
    starts = list(range(0, n_rows, slab))

    def copy(n):
        size = min(slab, n_rows - starts[n])
        return pltpu.make_async_copy(w_hbm.at[pl.ds(starts[n], size), :],
                                     stage_ref.at[n % 2, pl.ds(0, size), :], sem_ref.at[n % 2])

    copy(0).start()
    for n, start in enumerate(starts):
        if n + 1 < len(starts):
            copy(n + 1).start()
        copy(n).wait()
        size = min(slab, n_rows - start)
        consume(slice(start, start + size), stage_ref.at[n % 2, pl.ds(0, size), :])


def _stage_ffn_weights(win_hbm, wout_hbm, wg_ref, wu_ref, wo_ref, stage_in, stage_out, sem_in, sem_out):
    ff = wout_hbm.shape[0]
    ffp = wg_ref.shape[1]
    up_from = 2 * ff - ffp
    assert up_from % LANES == 0 and ff % BF16_SUBLANES == 0

    def consume_in(rows, blk):
        real = lax.broadcasted_iota(jnp.int32, (blk.shape[0], ffp), 1) < ff
        wg_ref[rows, :] = jnp.where(real, blk[:, :ffp], 0.0).astype(BF16)
        up = pltpu.roll(blk[:, up_from:], ffp - (ff - up_from), 1)
        wu_ref[rows, :] = jnp.where(real, up, 0.0).astype(BF16)

    def consume_out(rows, blk):
        wo_ref[rows, :] = blk[...].astype(BF16)

    _stream_rows(win_hbm, stage_in, sem_in, stage_in.shape[1], consume_in)
    _stream_rows(wout_hbm, stage_out, sem_out, stage_out.shape[1], consume_out)
    wo_ref[ff:, :] = jnp.zeros((ffp - ff, wo_ref.shape[1]), BF16)


def _ffn_kernel(x_ref, gn_ref, win_hbm, wout_hbm, o_ref, h_ref, a_ref,
                wg_ref, wu_ref, wo_ref, stage_in, stage_out, sem_in, sem_out):
    @pl.when(pl.program_id(0) == 0)
    def _():
        _stage_ffn_weights(win_hbm, wout_hbm, wg_ref, wu_ref, wo_ref, stage_in, stage_out, sem_in, sem_out)

    o_ref[...] = _swiglu_residual(x_ref[...], gn_ref, wg_ref, wu_ref, wo_ref, h_ref, a_ref)


def _ffn_call(x, gn, w_in, w_out):
    t, d = x.shape
    tm = FFN_TILE
    ff = w_out.shape[0]
    ffp = -(-ff // FF_CHUNK) * FF_CHUNK
    tok = pl.BlockSpec((tm, d), lambda i: (i, 0))
    hbm = pl.BlockSpec(memory_space=pl.ANY)
    return pl.pallas_call(
        _ffn_kernel,
        grid=(t // tm,),
        in_specs=[tok, _resident(gn.shape), hbm, hbm],
        out_specs=tok,
        out_shape=jax.ShapeDtypeStruct((t, d), F32),
        scratch_shapes=[pltpu.VMEM((tm, d), BF16), pltpu.VMEM((tm, ffp), BF16),
                        pltpu.VMEM((d, ffp), BF16), pltpu.VMEM((d, ffp), BF16), pltpu.VMEM((ffp, d), BF16),
                        pltpu.VMEM((2, STAGE_IN_ROWS, 2 * ff), F32), pltpu.VMEM((2, STAGE_OUT_ROWS, d), F32),
                        pltpu.SemaphoreType.DMA((2,)), pltpu.SemaphoreType.DMA((2,))],
        compiler_params=_cparams(1),
        name="ffn1",
    )(x, gn, w_in, w_out)


def _rope128(y, cos, sin_signed):
    return y * cos + pltpu.roll(y, 64, 1) * sin_signed


def _rope64(y, cos, sin_signed):
    first_half = (lax.broadcasted_iota(jnp.int32, y.shape, 1) // (ATT_HD // 2)) % 2 == 0
    partner = jnp.where(first_half, pltpu.roll(y, LANES - ATT_HD // 2, 1), pltpu.roll(y, ATT_HD // 2, 1))
    return y * cos + partner * sin_signed


def _proj_kernel(x_ref, gn_ref, w_hbm, cosr_ref, sinr_ref, cosa_ref, sina_ref,
                 qr_ref, kr_ref, vr_ref, sr_ref, qa_ref, k4_ref, va_ref, sgr_ref, sga_ref,
                 h_ref, w_ref, stage_ref, sem_ref, *, ret_w, att_qw):
    @pl.when(pl.program_id(0) == 0)
    def _():
        def consume(rows, blk):
            w_ref[rows, :] = blk[...].astype(BF16)
        _stream_rows(w_hbm, stage_ref, sem_ref, stage_ref.shape[1], consume)

    h_ref[...] = _rms(x_ref[...], gn_ref[...]).astype(BF16)

    def proj(a, b):
        return _dot(h_ref[...], w_ref[:, a:b])

    cosr, sinr = cosr_ref[...], sinr_ref[...]
    cosa, sina = cosa_ref[...], sina_ref[...]
    d = x_ref.shape[1]
    lane_blocks = [slice(j * LANES, (j + 1) * LANES) for j in range(ret_w // LANES)]
    assert att_qw == ret_w

    def ret_q(c0):
        y = proj(c0, c0 + ret_w)
        for sl in lane_blocks:
            qr_ref[:, sl] = _rope128(y[:, sl], cosr, sinr).astype(BF16)

    def ret_k(c0):
        y = proj(c0, c0 + ret_w)
        for sl in lane_blocks:
            kr_ref[sl, :] = (_rope128(y[:, sl], cosr, sinr) * (RET_DK ** -0.5)).T.astype(BF16)

    def ret_v(c0):
        vr_ref[...] = proj(c0, c0 + ret_w).astype(BF16)

    def ret_g(c0):
        sr_ref[...] = jax.nn.silu(proj(c0, c0 + ret_w)).astype(BF16)

    def att_q(c0):
        y = proj(c0, c0 + att_qw)
        for sl in lane_blocks:
            qa_ref[sl, :] = (_rope64(y[:, sl], cosa, sina) * (ATT_HD ** -0.5 * LOG2_E)).T.astype(BF16)

    def att_kv(c0):
        y = proj(c0, c0 + 2 * LANES)
        kk = _rope64(y[:, :LANES], cosa, sina)
        low = lax.broadcasted_iota(jnp.int32, kk.shape, 1) < ATT_HD
        kswap = pltpu.roll(kk, ATT_HD, 1)
        k4_ref[:, 0 * LANES:1 * LANES] = jnp.where(low, kk, 0.0).astype(BF16)
        k4_ref[:, 1 * LANES:2 * LANES] = jnp.where(low, 0.0, kswap).astype(BF16)
        k4_ref[:, 2 * LANES:3 * LANES] = jnp.where(low, kswap, 0.0).astype(BF16)
        k4_ref[:, 3 * LANES:4 * LANES] = jnp.where(low, 0.0, kk).astype(BF16)
        va_ref[...] = y[:, LANES:].T.astype(BF16)

    def gate_r(c0):
        sgr_ref[...] = jax.nn.sigmoid(proj(c0, c0 + d)).astype(BF16)

    def gate_a(c0):
        sga_ref[...] = jax.nn.sigmoid(proj(c0, c0 + d)).astype(BF16)

    groups = [(ret_q, ret_w), (ret_k, ret_w), (ret_v, ret_w), (ret_g, ret_w), (att_q, att_qw),
              (att_kv, 2 * LANES), (gate_r, d), (gate_a, d)]
    starts = np.cumsum([0] + [width for _, width in groups])
    for n in PROJ_RUN_ORDER:
        groups[n][0](int(starts[n]))


def _proj_call(x1, gn, w, cosr, sinr, cosa, sina, seq):
    t, d = x1.shape
    tm = PROJ_TILE
    ret_w = RET_HEADS * RET_DK
    att_qw = ATT_Q_HEADS * ATT_HD
    n_pos = seq // tm
    tok = lambda width: pl.BlockSpec((tm, width), lambda i: (i, 0))
    tab = pl.BlockSpec((tm, LANES), lambda i: (i % n_pos, 0))
    widths = (ret_w, ret_w, ret_w, ret_w, att_qw, 4 * LANES, LANES, d, d)
    out_specs = [tok(wd) for wd in widths]
    out_shape = [jax.ShapeDtypeStruct((t, wd), BF16) for wd in widths]
    for k_out in (1, 4, 6):
        out_specs[k_out] = pl.BlockSpec((widths[k_out], tm), lambda i: (0, i))
        out_shape[k_out] = jax.ShapeDtypeStruct((widths[k_out], t), BF16)
    return pl.pallas_call(
        functools.partial(_proj_kernel, ret_w=ret_w, att_qw=att_qw),
        grid=(t // tm,),
        in_specs=[tok(d), _resident(gn.shape), pl.BlockSpec(memory_space=pl.ANY), tab, tab, tab, tab],
        out_specs=out_specs,
        out_shape=out_shape,
        scratch_shapes=[pltpu.VMEM((tm, d), BF16), pltpu.VMEM(w.shape, BF16),
                        pltpu.VMEM((2, STAGE_IN_ROWS, w.shape[1]), F32), pltpu.SemaphoreType.DMA((2,))],
        compiler_params=_cparams(1),
        name="proj",
    )(x1, gn, w, cosr, sinr, cosa, sina)


def _ret_stages(h, dec_ref, q_ref, kt_ref, v_ref, s_ref, gain_ref, z_ref, kv_ref, st_ref):
    c = RET_TILE
    dk = kt_ref.shape[0]
    n_chunks = q_ref.shape[0] // c

    def log_gamma(direction, shape):
        return -jnp.exp(jnp.full(shape, dec_ref[direction, h], F32))

    def iota(shape, axis):
        return lax.broadcasted_iota(jnp.int32, shape, axis).astype(F32)

    rel = iota((c, c), 0) - iota((c, c), 1)
    decay = jnp.where(rel >= 0, jnp.exp(jnp.maximum(rel, 0.0) * log_gamma(0, (c, c))),
                      jnp.exp(jnp.maximum(-rel, 0.0) * log_gamma(1, (c, c))))
    pos = iota((c, dk), 0)
    kdec_f = jnp.exp((c - 1 - pos) * log_gamma(0, (c, dk)))
    kdec_b = jnp.exp(pos * log_gamma(1, (c, dk)))
    qdec_f = jnp.exp((pos + 1.0) * log_gamma(0, (c, dk)))
    qdec_b = jnp.exp((c - pos) * log_gamma(1, (c, dk)))
    g_f = jnp.exp(c * log_gamma(0, (dk, dk)))
    g_b = jnp.exp(c * log_gamma(1, (dk, dk)))

    def span(i):
        return slice(i * c, (i + 1) * c)

    def kv_stage(i):
        v = v_ref[span(i), :].astype(F32)
        rhs = jnp.concatenate([(v * kdec_f).astype(BF16), (v * kdec_b).astype(BF16)], axis=1)
        kv_ref[i] = _dot(kt_ref[:, span(i)], rhs)

    def state_stage():
        state = jnp.zeros((dk, dk), F32)
        for i in range(n_chunks):
            st_ref[i, :, :dk] = state.astype(BF16)
            state = g_f * state + kv_ref[i, :, :dk]
        state = jnp.zeros((dk, dk), F32)
        for i in reversed(range(n_chunks)):
            st_ref[i, :, dk:] = state.astype(BF16)
            state = g_b * state + kv_ref[i, :, dk:]

    def out_stage(i):
        q = q_ref[span(i), :]
        scores = _dot(q, kt_ref[:, span(i)])
        inner = _dot((scores * decay).astype(BF16), v_ref[span(i), :])
        cross = _dot(q, st_ref[i])
        y = inner + qdec_f * cross[:, :dk] + qdec_b * cross[:, dk:]
        mu = jnp.mean(y, axis=-1, keepdims=True)
        yc = y - mu
        var = jnp.mean(yc * yc, axis=-1, keepdims=True)
        yn = yc * lax.rsqrt(var + EPS) * gain_ref[...]
        z_ref[span(i), :] = (yn * s_ref[span(i), :].astype(F32)).astype(BF16)

    return ([functools.partial(kv_stage, i) for i in range(n_chunks)] + [state_stage]
            + [functools.partial(out_stage, i) for i in range(n_chunks)])


def _att_stages(tile, n_tiles, sink_ref, qt_ref, k_ref, kp_ref, kn_ref, vt_ref, vtp_ref, vtn_ref, ot_ref):
    blk = ATT_BLOCK
    n_blocks = qt_ref.shape[1] // blk
    grp = ATT_Q_HEADS // ATT_KV_HEADS
    hd = ATT_HD
    neg_inf = jnp.float32(-jnp.inf)
    kpos = lax.broadcasted_iota(jnp.int32, (blk, 2 * blk), 0)
    qpos = lax.broadcasted_iota(jnp.int32, (blk, 2 * blk), 1) % blk
    prev_bias = jnp.where(kpos >= qpos, 0.0, neg_inf)
    next_bias = jnp.where(kpos <= qpos, 0.0, neg_inf)
    first_prev_bias = prev_bias + jnp.where(tile > 0, 0.0, neg_inf)
    last_next_bias = next_bias + jnp.where(tile < n_tiles - 1, 0.0, neg_inf)
    left = lax.broadcasted_iota(jnp.int32, (1, 2 * blk), 1) < blk

    def cols(j):
        return slice(j * blk, (j + 1) * blk)

    def scores(j, g, p):
        qt = jnp.concatenate([qt_ref[cols(2 * g), cols(j)], qt_ref[cols(2 * g + 1), cols(j)]], axis=1)
        kv = cols(2 * g + p)
        kwin = jnp.concatenate([kp_ref[:, kv] if j == 0 else k_ref[cols(j - 1), kv], k_ref[cols(j), kv],
                                kn_ref[:, kv] if j == n_blocks - 1 else k_ref[cols(j + 1), kv]], axis=0)
        return _dot(kwin, qt)

    def finish(j, g, p, s):
        s_prev = s[cols(0), :] + (first_prev_bias if j == 0 else prev_bias)
        s_cur = s[cols(1), :]
        s_next = s[cols(2), :] + (last_next_bias if j == n_blocks - 1 else next_bias)
        snk = jnp.where(left, sink_ref[grp * g + p], sink_ref[grp * g + 2 + p]) * LOG2_E
        m = jnp.maximum(jnp.max(jnp.maximum(jnp.maximum(s_prev, s_cur), s_next), axis=0, keepdims=True), snk)
        et = jnp.concatenate([jnp.exp2(s_prev - m).astype(BF16), jnp.exp2(s_cur - m).astype(BF16),
                              jnp.exp2(s_next - m).astype(BF16)], axis=0)
        dims = slice(g * hd, (g + 1) * hd)
        vt = jnp.concatenate([vtp_ref[dims, :] if j == 0 else vt_ref[dims, cols(j - 1)],
                              vt_ref[dims, cols(j)],
                              vtn_ref[dims, :] if j == n_blocks - 1 else vt_ref[dims, cols(j + 1)]], axis=1)
        acc = _dot(jnp.concatenate([vt, jnp.ones((BF16_SUBLANES, 3 * blk), BF16)], axis=0), et)
        denom = acc[hd:hd + 1, :] + jnp.exp2(snk - m)
        ot = acc[:hd, :] * (1.0 / denom)
        head_a = grp * g + p
        head_b = grp * g + 2 + p
        ot_ref[head_a * hd:(head_a + 1) * hd, cols(j)] = ot[:, :blk].astype(BF16)
        ot_ref[head_b * hd:(head_b + 1) * hd, cols(j)] = ot[:, blk:].astype(BF16)

    units = [(j, g, p) for j in range(n_blocks) for g in range(ATT_KV_HEADS) for p in range(2)]
    pending = []

    def stage(n):
        if n == 0:
            pending.extend(scores(*u) for u in units[:ATT_SCORES_AHEAD])
        if n + ATT_SCORES_AHEAD < len(units):
            pending.append(scores(*units[n + ATT_SCORES_AHEAD]))
        finish(*units[n], pending.pop(0))

    return [functools.partial(stage, n) for n in range(len(units))]


def _interleave(a, b):
    if len(a) < len(b):
        a, b = b, a
    out, taken = [], 0
    for n, stage in enumerate(a):
        out.append(stage)
        due = (n + 1) * len(b) // len(a)
        out += b[taken:due]
        taken = due
    return out


def _mixer_kernel(dec_ref, sink_ref, q_ref, kt_ref, v_ref, s_ref, gain_ref,
                  qt_ref, k_ref, kp_ref, kn_ref, vt_ref, vtp_ref, vtn_ref,
                  z_ref, ot_ref, kv_ref, st_ref):
    j = pl.program_id(1)
    ret = _ret_stages(j, dec_ref, q_ref, kt_ref, v_ref, s_ref, gain_ref, z_ref, kv_ref, st_ref)
    att = _att_stages(j, pl.num_programs(1), sink_ref, qt_ref, k_ref, kp_ref, kn_ref,
                      vt_ref, vtp_ref, vtn_ref, ot_ref)
    for stage in _interleave(ret, att):
        stage()


def _mixer_call(dec, sink, qr, krt, vr, sr, gain, qat, k4, vat, batch, seq):
    t, ret_w = qr.shape
    att_w = qat.shape[0]
    tq = ATT_TILE
    n_tiles = seq // tq
    assert n_tiles == RET_HEADS, "one attention tile per retention head in each grid step"
    per_tile = tq // ATT_BLOCK
    n_halo = seq // ATT_BLOCK
    n_chunks = seq // RET_TILE

    def prev_blk(b, i):
        return b * n_halo + jnp.maximum(i * per_tile - 1, 0)

    def next_blk(b, i):
        return b * n_halo + jnp.minimum((i + 1) * per_tile, n_halo - 1)

    def dims_by_tokens(rows):
        return (pl.BlockSpec((rows, tq), lambda b, i: (0, b * n_tiles + i)),
                pl.BlockSpec((rows, ATT_BLOCK), lambda b, i: (0, prev_blk(b, i))),
                pl.BlockSpec((rows, ATT_BLOCK), lambda b, i: (0, next_blk(b, i))))

    kw = k4.shape[1]
    smem = pl.BlockSpec(memory_space=pltpu.SMEM)
    head = pl.BlockSpec((seq, RET_DK), lambda b, h: (b, h))
    return pl.pallas_call(
        _mixer_kernel,
        grid=(batch, n_tiles),
        in_specs=[smem, smem, head, pl.BlockSpec((RET_DK, seq), lambda b, h: (h, b)), head, head,
                  pl.BlockSpec((1, RET_DK), lambda b, h: (0, h)),
                  dims_by_tokens(att_w)[0],
                  pl.BlockSpec((tq, kw), lambda b, i: (b * n_tiles + i, 0)),
                  pl.BlockSpec((ATT_BLOCK, kw), lambda b, i: (prev_blk(b, i), 0)),
                  pl.BlockSpec((ATT_BLOCK, kw), lambda b, i: (next_blk(b, i), 0)),
                  *dims_by_tokens(vat.shape[0])],
        out_specs=[head, dims_by_tokens(att_w)[0]],
        out_shape=[jax.ShapeDtypeStruct((t, ret_w), BF16), jax.ShapeDtypeStruct((att_w, t), BF16)],
        scratch_shapes=[pltpu.VMEM((n_chunks, RET_DK, 2 * RET_DK), F32),
                        pltpu.VMEM((n_chunks, RET_DK, 2 * RET_DK), BF16)],
        compiler_params=_cparams(2),
        name="mixers",
    )(dec, sink, qr, krt, vr, sr, gain, qat, k4, k4, k4, vat, vat, vat)


def _tail_kernel(x_ref, zr_ref, oa_ref, sgr_ref, sga_ref, wro_ref, wao_ref, wout_ref,
                 gn_ref, wg_ref, wu_ref, wo_ref, gfin_ref, y_ref, h_ref, a_ref):
    y_ret = _dot(zr_ref[...], wro_ref[...])
    y_att = lax.dot_general(oa_ref[...], wao_ref[...], (((0,), (0,)), ((), ())), preferred_element_type=F32)
    merged = sgr_ref[...].astype(F32) * y_ret + sga_ref[...].astype(F32) * y_att
    x2 = x_ref[...] + _dot(merged.astype(BF16), wout_ref[...])
    group = h_ref.shape[0]
    for r in range(x2.shape[0] // group):
        rows = slice(r * group, (r + 1) * group)
        x3 = _swiglu_residual(x2[rows, :], gn_ref, wg_ref, wu_ref, wo_ref, h_ref, a_ref)
        y_ref[rows, :] = _rms(x3, gfin_ref[...])


def _tail_call(x1, zr, oa, sgr, sga, wro, wao, wout, gn, wg, wu, wo, gfin):
    t, d = x1.shape
    tm = TAIL_TILE
    ffp = wo.shape[0]
    tok = lambda width: pl.BlockSpec((tm, width), lambda i: (i, 0))
    return pl.pallas_call(
        _tail_kernel,
        grid=(t // tm,),
        in_specs=[tok(d), tok(zr.shape[1]), pl.BlockSpec((oa.shape[0], tm), lambda i: (0, i)), tok(d), tok(d),
                  _resident(wro.shape), _resident(wao.shape), _resident(wout.shape),
                  _resident(gn.shape), _resident(wg.shape), _resident(wu.shape), _resident(wo.shape),
                  _resident(gfin.shape)],
        out_specs=tok(d),
        out_shape=jax.ShapeDtypeStruct((t, d), F32),
        scratch_shapes=[pltpu.VMEM((TAIL_FFN_ROWS, d), BF16), pltpu.VMEM((TAIL_FFN_ROWS, ffp), BF16)],
        compiler_params=_cparams(1),
        name="tail",
    )(x1, zr, oa, sgr, sga, wro, wao, wout, gn, wg, wu, wo, gfin)


def _prep_ffn(w_in, w_out):
    d, two_ff = w_in.shape
    ff = two_ff // 2
    ffp = -(-ff // FF_CHUNK) * FF_CHUNK
    pad = ((0, 0), (0, ffp - ff))
    wg = jnp.pad(w_in[:, :ff].astype(BF16), pad)
    wu = jnp.pad(w_in[:, ff:].astype(BF16), pad)
    wo = jnp.pad(w_out.astype(BF16), ((0, ffp - ff), (0, 0)))
    return wg, wu, wo


def _rope_tables(seq):
    pos = np.arange(seq, dtype=np.float32)[:, None]

    def cs(half):
        inv_freq = np.float32(ROPE_THETA) ** (-np.arange(half, dtype=np.float32) / np.float32(half))
        ang = (pos * inv_freq[None, :].astype(np.float32)).astype(np.float64)
        return np.cos(ang).astype(np.float32), np.sin(ang).astype(np.float32)

    c, s = cs(RET_DK // 2)
    cosr = np.concatenate([c, c], axis=1)
    sinr = np.concatenate([-s, s], axis=1)
    c, s = cs(ATT_HD // 2)
    cosa = np.concatenate([c, c, c, c], axis=1)
    sina = np.concatenate([-s, s, -s, s], axis=1)
    return tuple(jnp.asarray(tab) for tab in (cosr, sinr, cosa, sina))


def kernel(x, norm_ffn1, ffn1_w_in, ffn1_w_out, norm_mix, w_in, ret_decay_fwd, ret_decay_bwd,
           ret_gn_gain, w_ret_out, att_sink, w_att_out, w_out, norm_ffn2, ffn2_w_in, ffn2_w_out,
           norm_final):
    b, s, d = x.shape
    assert ffn1_w_in.shape[0] == 1, "single-layer block"
    assert all(s % tile == 0 for tile in (TAIL_TILE, FFN_TILE, PROJ_TILE, RET_TILE, ATT_TILE))
    assert TAIL_TILE % TAIL_FFN_ROWS == 0
    xt = x.reshape(b * s, d)
    ffn2_w = _prep_ffn(ffn2_w_in[0], ffn2_w_out[0])
    cosr, sinr, cosa, sina = _rope_tables(s)

    x1 = _ffn_call(xt, norm_ffn1, ffn1_w_in[0], ffn1_w_out[0])
    qr, krt, vr, sr, qat, k4, vat, sgr, sga = _proj_call(
        x1, norm_mix, w_in[0], cosr, sinr, cosa, sina, s)
    dec = jnp.concatenate([ret_decay_fwd, ret_decay_bwd], axis=0)
    zr, oat = _mixer_call(dec, att_sink[0], qr, krt, vr, sr, ret_gn_gain, qat, k4, vat, b, s)
    y = _tail_call(x1, zr, oat, sgr, sga, w_ret_out[0].astype(BF16), w_att_out[0].astype(BF16),
                   w_out[0].astype(BF16), norm_ffn2, *ffn2_w, norm_final.reshape(1, d))
    return y.reshape(b, s, d)
```

```python
import functools

import jax
import jax.numpy as jnp
import numpy as np
from jax import lax
from jax.experimental import pallas as pl
from jax.experimental.pallas import tpu as pltpu

F32 = jnp.float32
BF16 = jnp.bfloat16

EPS = 1e-6
LOG2_E = 1.4426950408889634
ROPE_THETA = 10000.0
RET_HEADS = 4
RET_DK = 128
ATT_Q_HEADS = 8
ATT_KV_HEADS = 2
ATT_HD = 64
ATT_BLOCK = 128

LANES = 128
BF16_SUBLANES = 16
MXU_WIDTH = 256
V7X_VMEM_BYTES = 64 * 1024 * 1024
VMEM_LIMIT = V7X_VMEM_BYTES - 6 * 1024 * 1024

FF_CHUNK = MXU_WIDTH
RET_TILE = 256
ATT_TILE = 1024
ATT_SCORES_AHEAD = 3
FFN_TILE = 1024
STAGE_IN_ROWS = 64
STAGE_OUT_ROWS = 256
PROJ_TILE = 1024
PROJ_RUN_ORDER = (6, 0, 7, 1, 3, 4, 5, 2)
TAIL_TILE = 1024
TAIL_FFN_ROWS = 512


def _cparams(n_grid):
    return pltpu.CompilerParams(dimension_semantics=("arbitrary",) * n_grid,
                                vmem_limit_bytes=VMEM_LIMIT)


def _resident(shape):
    zeros = (0,) * len(shape)
    return pl.BlockSpec(shape, lambda *_: zeros, pipeline_mode=pl.Buffered(1))


def _rms(x, g):
    y = x * lax.rsqrt(jnp.mean(x * x, axis=-1, keepdims=True) + EPS)
    return y * g


def _dot(a, b):
    return jnp.dot(a, b, preferred_element_type=F32)


def _swiglu_residual(x, gn_ref, wg_ref, wu_ref, wo_ref, h_ref, a_ref):
    h_ref[...] = _rms(x, gn_ref[...]).astype(BF16)
    fc = FF_CHUNK
    for c in range(wg_ref.shape[1] // fc):
        chunk = slice(c * fc, (c + 1) * fc)
        gu = _dot(h_ref[...], jnp.concatenate([wg_ref[:, chunk], wu_ref[:, chunk]], axis=1))
        a_ref[:, chunk] = (jax.nn.silu(gu[:, :fc]) * gu[:, fc:]).astype(BF16)
    return x + 0.5 * _dot(a_ref[...], wo_ref[...])


def _stream_rows(w_hbm, stage_ref, sem_ref, slab, consume):
    n_rows = w_hbm.shape[0]
    starts = list(range(0, n_rows, slab))

    def copy(n):
        size = min(slab, n_rows - starts[n])
        return pltpu.make_async_copy(w_hbm.at[pl.ds(starts[n], size), :],
                                     stage_ref.at[n % 2, pl.ds(0, size), :], sem_ref.at[n % 2])

    copy(0).start()
    for n, start in enumerate(starts):
        if n + 1 < len(starts):
            copy(n + 1).start()
        copy(n).wait()
        size = min(slab, n_rows - start)
        consume(slice(start, start + size), stage_ref.at[n % 2, pl.ds(0, size), :])


def _padded_ff(ff):
    return -(-ff // FF_CHUNK) * FF_CHUNK


def _split_gate_up(w, ff):
    ffp = _padded_ff(ff)
    up_from = 2 * ff - ffp
    assert up_from % LANES == 0
    real = lax.broadcasted_iota(jnp.int32, (w.shape[0], ffp), 1) < ff
    gate = jnp.where(real, w[:, :ffp], 0.0).astype(BF16)
    up = pltpu.roll(w[:, up_from:], ffp - (ff - up_from), 1)
    return gate, jnp.where(real, up, 0.0).astype(BF16)


def _stage_ffn_weights(win_hbm, wout_hbm, wg_ref, wu_ref, wo_ref, stage_in, stage_out, sem_in, sem_out):
    ff = wout_hbm.shape[0]
    assert ff % BF16_SUBLANES == 0

    def consume_in(rows, blk):
        wg_ref[rows, :], wu_ref[rows, :] = _split_gate_up(blk[...], ff)

    def consume_out(rows, blk):
        wo_ref[rows, :] = blk[...].astype(BF16)

    _stream_rows(win_hbm, stage_in, sem_in, stage_in.shape[1], consume_in)
    _stream_rows(wout_hbm, stage_out, sem_out, stage_out.shape[1], consume_out)
    wo_ref[ff:, :] = jnp.zeros((wo_ref.shape[0] - ff, wo_ref.shape[1]), BF16)


def _ffn_kernel(x_ref, gn_ref, win_hbm, wout_hbm, win2_ref, wout2_ref, *rest, ff, n_plain):
    plain_in, rest = rest[:n_plain], rest[n_plain:]
    o_ref, wg2_ref, wu2_ref, wo2_ref = rest[:4]
    plain_out, rest = rest[4:4 + n_plain], rest[4 + n_plain:]
    h_ref, a_ref, wg_ref, wu_ref, wo_ref, stage_in, stage_out, sem_in, sem_out = rest

    @pl.when(pl.program_id(0) == 0)
    def _():
        _stage_ffn_weights(win_hbm, wout_hbm, wg_ref, wu_ref, wo_ref, stage_in, stage_out, sem_in, sem_out)

    o_ref[...] = _swiglu_residual(x_ref[...], gn_ref, wg_ref, wu_ref, wo_ref, h_ref, a_ref)

    wg2_ref[...], wu2_ref[...] = _split_gate_up(win2_ref[...], ff)
    slab = wout2_ref.shape[0]
    row = (pl.program_id(0) // 2) * slab + lax.broadcasted_iota(jnp.int32, wout2_ref.shape, 0)
    wo2_ref[...] = jnp.where(row < ff, wout2_ref[...], 0.0).astype(BF16)
    for src, dst in zip(plain_in, plain_out):
        dst[...] = src[...].astype(BF16)


def _ffn_call(x, gn, w_in, w_out, w_in2, w_out2, plain):
    t, d = x.shape
    tm = FFN_TILE
    n = t // tm
    ff = w_out.shape[0]
    ffp = _padded_ff(ff)
    assert n % 2 == 0 and w_in2.shape == w_in.shape and w_out2.shape == w_out.shape
    tok = pl.BlockSpec((tm, d), lambda i: (i, 0))
    hbm = pl.BlockSpec(memory_space=pl.ANY)

    def slab(rows, width):
        assert rows % n == 0 and (rows // n) % BF16_SUBLANES == 0
        return pl.BlockSpec((rows // n, width), lambda i: (i, 0))

    down_rows = ffp // (n // 2)
    assert ffp % (n // 2) == 0 and down_rows % BF16_SUBLANES == 0
    down = pl.BlockSpec((down_rows, d), lambda i: (i // 2, 0))
    plain_specs = [slab(*w.shape) for w in plain]
    outs = pl.pallas_call(
        functools.partial(_ffn_kernel, ff=ff, n_plain=len(plain)),
        grid=(n,),
        in_specs=[tok, _resident(gn.shape), hbm, hbm, slab(d, 2 * ff), down] + plain_specs,
        out_specs=[tok, slab(d, ffp), slab(d, ffp), down] + plain_specs,
        out_shape=[jax.ShapeDtypeStruct((t, d), F32), jax.ShapeDtypeStruct((d, ffp), BF16),
                   jax.ShapeDtypeStruct((d, ffp), BF16), jax.ShapeDtypeStruct((ffp, d), BF16)]
                  + [jax.ShapeDtypeStruct(w.shape, BF16) for w in plain],
        scratch_shapes=[pltpu.VMEM((tm, d), BF16), pltpu.VMEM((tm, ffp), BF16),
                        pltpu.VMEM((d, ffp), BF16), pltpu.VMEM((d, ffp), BF16), pltpu.VMEM((ffp, d), BF16),
                        pltpu.VMEM((2, STAGE_IN_ROWS, 2 * ff), F32), pltpu.VMEM((2, STAGE_OUT_ROWS, d), F32),
                        pltpu.SemaphoreType.DMA((2,)), pltpu.SemaphoreType.DMA((2,))],
        compiler_params=_cparams(1),
        name="ffn1",
    )(x, gn, w_in, w_out, w_in2, w_out2, *plain)
    return outs[0], tuple(outs[1:4]), tuple(outs[4:])


def _rope128(y, cos, sin_signed):
    return y * cos + pltpu.roll(y, 64, 1) * sin_signed


def _rope64(y, cos, sin_signed):
    first_half = (lax.broadcasted_iota(jnp.int32, y.shape, 1) // (ATT_HD // 2)) % 2 == 0
    partner = jnp.where(first_half, pltpu.roll(y, LANES - ATT_HD // 2, 1), pltpu.roll(y, ATT_HD // 2, 1))
    return y * cos + partner * sin_signed


def _proj_kernel(x_ref, gn_ref, w_ref, cosr_ref, sinr_ref, cosa_ref, sina_ref,
                 qr_ref, kr_ref, vr_ref, sr_ref, qa_ref, k4_ref, va_ref, sgr_ref, sga_ref,
                 h_ref, *, ret_w, att_qw):
    h_ref[...] = _rms(x_ref[...], gn_ref[...]).astype(BF16)

    def proj(a, b):
        return _dot(h_ref[...], w_ref[:, a:b])

    cosr, sinr = cosr_ref[...], sinr_ref[...]
    cosa, sina = cosa_ref[...], sina_ref[...]
    d = x_ref.shape[1]
    lane_blocks = [slice(j * LANES, (j + 1) * LANES) for j in range(ret_w // LANES)]
    assert att_qw == ret_w

    def ret_q(c0):
        y = proj(c0, c0 + ret_w)
        for sl in lane_blocks:
            qr_ref[:, sl] = _rope128(y[:, sl], cosr, sinr).astype(BF16)

    def ret_k(c0):
        y = proj(c0, c0 + ret_w)
        for sl in lane_blocks:
            kr_ref[sl, :] = (_rope128(y[:, sl], cosr, sinr) * (RET_DK ** -0.5)).T.astype(BF16)

    def ret_v(c0):
        vr_ref[...] = proj(c0, c0 + ret_w).astype(BF16)

    def ret_g(c0):
        sr_ref[...] = jax.nn.silu(proj(c0, c0 + ret_w)).astype(BF16)

    def att_q(c0):
        y = proj(c0, c0 + att_qw)
        for sl in lane_blocks:
            qa_ref[sl, :] = (_rope64(y[:, sl], cosa, sina) * (ATT_HD ** -0.5 * LOG2_E)).T.astype(BF16)

    def att_kv(c0):
        y = proj(c0, c0 + 2 * LANES)
        kk = _rope64(y[:, :LANES], cosa, sina)
        low = lax.broadcasted_iota(jnp.int32, kk.shape, 1) < ATT_HD
        kswap = pltpu.roll(kk, ATT_HD, 1)
        k4_ref[:, 0 * LANES:1 * LANES] = jnp.where(low, kk, 0.0).astype(BF16)
        k4_ref[:, 1 * LANES:2 * LANES] = jnp.where(low, 0.0, kswap).astype(BF16)
        k4_ref[:, 2 * LANES:3 * LANES] = jnp.where(low, kswap, 0.0).astype(BF16)
        k4_ref[:, 3 * LANES:4 * LANES] = jnp.where(low, 0.0, kk).astype(BF16)
        va_ref[...] = y[:, LANES:].T.astype(BF16)

    def gate_r(c0):
        sgr_ref[...] = jax.nn.sigmoid(proj(c0, c0 + d)).astype(BF16)

    def gate_a(c0):
        sga_ref[...] = jax.nn.sigmoid(proj(c0, c0 + d)).astype(BF16)

    groups = [(ret_q, ret_w), (ret_k, ret_w), (ret_v, ret_w), (ret_g, ret_w), (att_q, att_qw),
              (att_kv, 2 * LANES), (gate_r, d), (gate_a, d)]
    starts = np.cumsum([0] + [width for _, width in groups])
    for n in PROJ_RUN_ORDER:
        groups[n][0](int(starts[n]))


def _proj_call(x1, gn, w, cosr, sinr, cosa, sina, seq):
    t, d = x1.shape
    tm = PROJ_TILE
    ret_w = RET_HEADS * RET_DK
    att_qw = ATT_Q_HEADS * ATT_HD
    n_pos = seq // tm
    tok = lambda width: pl.BlockSpec((tm, width), lambda i: (i, 0))
    tab = pl.BlockSpec((tm, LANES), lambda i: (i % n_pos, 0))
    widths = (ret_w, ret_w, ret_w, ret_w, att_qw, 4 * LANES, LANES, d, d)
    out_specs = [tok(wd) for wd in widths]
    out_shape = [jax.ShapeDtypeStruct((t, wd), BF16) for wd in widths]
    for k_out in (1, 4, 6):
        out_specs[k_out] = pl.BlockSpec((widths[k_out], tm), lambda i: (0, i))
        out_shape[k_out] = jax.ShapeDtypeStruct((widths[k_out], t), BF16)
    return pl.pallas_call(
        functools.partial(_proj_kernel, ret_w=ret_w, att_qw=att_qw),
        grid=(t // tm,),
        in_specs=[tok(d), _resident(gn.shape), _resident(w.shape), tab, tab, tab, tab],
        out_specs=out_specs,
        out_shape=out_shape,
        scratch_shapes=[pltpu.VMEM((tm, d), BF16)],
        compiler_params=_cparams(1),
        name="proj",
    )(x1, gn, w, cosr, sinr, cosa, sina)


def _ret_stages(h, dec_ref, q_ref, kt_ref, v_ref, s_ref, gain_ref, z_ref, kv_ref, st_ref):
    c = RET_TILE
    dk = kt_ref.shape[0]
    n_chunks = q_ref.shape[0] // c

    def log_gamma(direction, shape):
        return -jnp.exp(jnp.full(shape, dec_ref[direction, h], F32))

    def iota(shape, axis):
        return lax.broadcasted_iota(jnp.int32, shape, axis).astype(F32)

    rel = iota((c, c), 0) - iota((c, c), 1)
    decay = jnp.where(rel >= 0, jnp.exp(jnp.maximum(rel, 0.0) * log_gamma(0, (c, c))),
                      jnp.exp(jnp.maximum(-rel, 0.0) * log_gamma(1, (c, c))))
    pos = iota((c, dk), 0)
    kdec_f = jnp.exp((c - 1 - pos) * log_gamma(0, (c, dk)))
    kdec_b = jnp.exp(pos * log_gamma(1, (c, dk)))
    qdec_f = jnp.exp((pos + 1.0) * log_gamma(0, (c, dk)))
    qdec_b = jnp.exp((c - pos) * log_gamma(1, (c, dk)))
    g_f = jnp.exp(c * log_gamma(0, (dk, dk)))
    g_b = jnp.exp(c * log_gamma(1, (dk, dk)))

    def span(i):
        return slice(i * c, (i + 1) * c)

    def kv_stage(i):
        v = v_ref[span(i), :].astype(F32)
        rhs = jnp.concatenate([(v * kdec_f).astype(BF16), (v * kdec_b).astype(BF16)], axis=1)
        kv_ref[i] = _dot(kt_ref[:, span(i)], rhs)

    def state_stage():
        state = jnp.zeros((dk, dk), F32)
        for i in range(n_chunks):
            st_ref[i, :, :dk] = state.astype(BF16)
            state = g_f * state + kv_ref[i, :, :dk]
        state = jnp.zeros((dk, dk), F32)
        for i in reversed(range(n_chunks)):
            st_ref[i, :, dk:] = state.astype(BF16)
            state = g_b * state + kv_ref[i, :, dk:]

    def out_stage(i):
        q = q_ref[span(i), :]
        scores = _dot(q, kt_ref[:, span(i)])
        inner = _dot((scores * decay).astype(BF16), v_ref[span(i), :])
        cross = _dot(q, st_ref[i])
        y = inner + qdec_f * cross[:, :dk] + qdec_b * cross[:, dk:]
        mu = jnp.mean(y, axis=-1, keepdims=True)
        yc = y - mu
        var = jnp.mean(yc * yc, axis=-1, keepdims=True)
        yn = yc * lax.rsqrt(var + EPS) * gain_ref[...]
        z_ref[span(i), :] = (yn * s_ref[span(i), :].astype(F32)).astype(BF16)

    return ([functools.partial(kv_stage, i) for i in range(n_chunks)] + [state_stage]
            + [functools.partial(out_stage, i) for i in range(n_chunks)])


def _att_stages(tile, n_tiles, sink_ref, qt_ref, k_ref, kp_ref, kn_ref, vt_ref, vtp_ref, vtn_ref, ot_ref):
    blk = ATT_BLOCK
    n_blocks = qt_ref.shape[1] // blk
    grp = ATT_Q_HEADS // ATT_KV_HEADS
    hd = ATT_HD
    neg_inf = jnp.float32(-jnp.inf)
    kpos = lax.broadcasted_iota(jnp.int32, (blk, 2 * blk), 0)
    qpos = lax.broadcasted_iota(jnp.int32, (blk, 2 * blk), 1) % blk
    prev_bias = jnp.where(kpos >= qpos, 0.0, neg_inf)
    next_bias = jnp.where(kpos <= qpos, 0.0, neg_inf)
    first_prev_bias = prev_bias + jnp.where(tile > 0, 0.0, neg_inf)
    last_next_bias = next_bias + jnp.where(tile < n_tiles - 1, 0.0, neg_inf)
    left = lax.broadcasted_iota(jnp.int32, (1, 2 * blk), 1) < blk

    def cols(j):
        return slice(j * blk, (j + 1) * blk)

    def scores(j, g, p):
        qt = jnp.concatenate([qt_ref[cols(2 * g), cols(j)], qt_ref[cols(2 * g + 1), cols(j)]], axis=1)
        kv = cols(2 * g + p)
        kwin = jnp.concatenate([kp_ref[:, kv] if j == 0 else k_ref[cols(j - 1), kv], k_ref[cols(j), kv],
                                kn_ref[:, kv] if j == n_blocks - 1 else k_ref[cols(j + 1), kv]], axis=0)
        return _dot(kwin, qt)

    def finish(j, g, p, s):
        s_prev = s[cols(0), :] + (first_prev_bias if j == 0 else prev_bias)
        s_cur = s[cols(1), :]
        s_next = s[cols(2), :] + (last_next_bias if j == n_blocks - 1 else next_bias)
        snk = jnp.where(left, sink_ref[grp * g + p], sink_ref[grp * g + 2 + p]) * LOG2_E
        m = jnp.maximum(jnp.max(jnp.maximum(jnp.maximum(s_prev, s_cur), s_next), axis=0, keepdims=True), snk)
        et = jnp.concatenate([jnp.exp2(s_prev - m).astype(BF16), jnp.exp2(s_cur - m).astype(BF16),
                              jnp.exp2(s_next - m).astype(BF16)], axis=0)
        dims = slice(g * hd, (g + 1) * hd)
        vt = jnp.concatenate([vtp_ref[dims, :] if j == 0 else vt_ref[dims, cols(j - 1)],
                              vt_ref[dims, cols(j)],
                              vtn_ref[dims, :] if j == n_blocks - 1 else vt_ref[dims, cols(j + 1)]], axis=1)
        acc = _dot(jnp.concatenate([vt, jnp.ones((BF16_SUBLANES, 3 * blk), BF16)], axis=0), et)
        denom = acc[hd:hd + 1, :] + jnp.exp2(snk - m)
        ot = acc[:hd, :] * (1.0 / denom)
        head_a = grp * g + p
        head_b = grp * g + 2 + p
        ot_ref[head_a * hd:(head_a + 1) * hd, cols(j)] = ot[:, :blk].astype(BF16)
        ot_ref[head_b * hd:(head_b + 1) * hd, cols(j)] = ot[:, blk:].astype(BF16)

    units = [(j, g, p) for j in range(n_blocks) for g in range(ATT_KV_HEADS) for p in range(2)]
    pending = []

    def stage(n):
        if n == 0:
            pending.extend(scores(*u) for u in units[:ATT_SCORES_AHEAD])
        if n + ATT_SCORES_AHEAD < len(units):
            pending.append(scores(*units[n + ATT_SCORES_AHEAD]))
        finish(*units[n], pending.pop(0))

    return [functools.partial(stage, n) for n in range(len(units))]


def _interleave(a, b):
    if len(a) < len(b):
        a, b = b, a
    out, taken = [], 0
    for n, stage in enumerate(a):
        out.append(stage)
        due = (n + 1) * len(b) // len(a)
        out += b[taken:due]
        taken = due
    return out


def _mixer_kernel(dec_ref, sink_ref, q_ref, kt_ref, v_ref, s_ref, gain_ref,
                  qt_ref, k_ref, kp_ref, kn_ref, vt_ref, vtp_ref, vtn_ref,
                  z_ref, ot_ref, kv_ref, st_ref):
    j = pl.program_id(1)
    ret = _ret_stages(j, dec_ref, q_ref, kt_ref, v_ref, s_ref, gain_ref, z_ref, kv_ref, st_ref)
    att = _att_stages(j, pl.num_programs(1), sink_ref, qt_ref, k_ref, kp_ref, kn_ref,
                      vt_ref, vtp_ref, vtn_ref, ot_ref)
    for stage in _interleave(ret, att):
        stage()


def _mixer_call(dec, sink, qr, krt, vr, sr, gain, qat, k4, vat, batch, seq):
    t, ret_w = qr.shape
    att_w = qat.shape[0]
    tq = ATT_TILE
    n_tiles = seq // tq
    assert n_tiles == RET_HEADS, "one attention tile per retention head in each grid step"
    per_tile = tq // ATT_BLOCK
    n_halo = seq // ATT_BLOCK
    n_chunks = seq // RET_TILE

    def prev_blk(b, i):
        return b * n_halo + jnp.maximum(i * per_tile - 1, 0)

    def next_blk(b, i):
        return b * n_halo + jnp.minimum((i + 1) * per_tile, n_halo - 1)

    def dims_by_tokens(rows):
        return (pl.BlockSpec((rows, tq), lambda b, i: (0, b * n_tiles + i)),
                pl.BlockSpec((rows, ATT_BLOCK), lambda b, i: (0, prev_blk(b, i))),
                pl.BlockSpec((rows, ATT_BLOCK), lambda b, i: (0, next_blk(b, i))))

    kw = k4.shape[1]
    smem = pl.BlockSpec(memory_space=pltpu.SMEM)
    head = pl.BlockSpec((seq, RET_DK), lambda b, h: (b, h))
    return pl.pallas_call(
        _mixer_kernel,
        grid=(batch, n_tiles),
        in_specs=[smem, smem, head, pl.BlockSpec((RET_DK, seq), lambda b, h: (h, b)), head, head,
                  pl.BlockSpec((1, RET_DK), lambda b, h: (0, h)),
                  dims_by_tokens(att_w)[0],
                  pl.BlockSpec((tq, kw), lambda b, i: (b * n_tiles + i, 0)),
                  pl.BlockSpec((ATT_BLOCK, kw), lambda b, i: (prev_blk(b, i), 0)),
                  pl.BlockSpec((ATT_BLOCK, kw), lambda b, i: (next_blk(b, i), 0)),
                  *dims_by_tokens(vat.shape[0])],
        out_specs=[head, dims_by_tokens(att_w)[0]],
        out_shape=[jax.ShapeDtypeStruct((t, ret_w), BF16), jax.ShapeDtypeStruct((att_w, t), BF16)],
        scratch_shapes=[pltpu.VMEM((n_chunks, RET_DK, 2 * RET_DK), F32),
                        pltpu.VMEM((n_chunks, RET_DK, 2 * RET_DK), BF16)],
        compiler_params=_cparams(2),
        name="mixers",
    )(dec, sink, qr, krt, vr, sr, gain, qat, k4, k4, k4, vat, vat, vat)


def _tail_kernel(x_ref, zr_ref, oa_ref, sgr_ref, sga_ref, wro_ref, wao_ref, wout_ref,
                 gn_ref, wg_ref, wu_ref, wo_ref, gfin_ref, y_ref, h_ref, a_ref):
    y_ret = _dot(zr_ref[...], wro_ref[...])
    y_att = lax.dot_general(oa_ref[...], wao_ref[...], (((0,), (0,)), ((), ())), preferred_element_type=F32)
    merged = sgr_ref[...].astype(F32) * y_ret + sga_ref[...].astype(F32) * y_att
    x2 = x_ref[...] + _dot(merged.astype(BF16), wout_ref[...])
    group = h_ref.shape[0]
    for r in range(x2.shape[0] // group):
        rows = slice(r * group, (r + 1) * group)
        x3 = _swiglu_residual(x2[rows, :], gn_ref, wg_ref, wu_ref, wo_ref, h_ref, a_ref)
        y_ref[rows, :] = _rms(x3, gfin_ref[...])


def _tail_call(x1, zr, oa, sgr, sga, wro, wao, wout, gn, wg, wu, wo, gfin):
    t, d = x1.shape
    tm = TAIL_TILE
    ffp = wo.shape[0]
    tok = lambda width: pl.BlockSpec((tm, width), lambda i: (i, 0))
    return pl.pallas_call(
        _tail_kernel,
        grid=(t // tm,),
        in_specs=[tok(d), tok(zr.shape[1]), pl.BlockSpec((oa.shape[0], tm), lambda i: (0, i)), tok(d), tok(d),
                  _resident(wro.shape), _resident(wao.shape), _resident(wout.shape),
                  _resident(gn.shape), _resident(wg.shape), _resident(wu.shape), _resident(wo.shape),
                  _resident(gfin.shape)],
        out_specs=tok(d),
        out_shape=jax.ShapeDtypeStruct((t, d), F32),
        scratch_shapes=[pltpu.VMEM((TAIL_FFN_ROWS, d), BF16), pltpu.VMEM((TAIL_FFN_ROWS, ffp), BF16)],
        compiler_params=_cparams(1),
        name="tail",
    )(x1, zr, oa, sgr, sga, wro, wao, wout, gn, wg, wu, wo, gfin)


def _rope_tables(seq):
    pos = np.arange(seq, dtype=np.float32)[:, None]

    def cs(half):
        inv_freq = np.float32(ROPE_THETA) ** (-np.arange(half, dtype=np.float32) / np.float32(half))
        ang = (pos * inv_freq[None, :].astype(np.float32)).astype(np.float64)
        return np.cos(ang).astype(np.float32), np.sin(ang).astype(np.float32)

    c, s = cs(RET_DK // 2)
    cosr = np.concatenate([c, c], axis=1)
    sinr = np.concatenate([-s, s], axis=1)
    c, s = cs(ATT_HD // 2)
    cosa = np.concatenate([c, c, c, c], axis=1)
    sina = np.concatenate([-s, s, -s, s], axis=1)
    return tuple(jnp.asarray(tab) for tab in (cosr, sinr, cosa, sina))


def kernel(x, norm_ffn1, ffn1_w_in, ffn1_w_out, norm_mix, w_in, ret_decay_fwd, ret_decay_bwd,
           ret_gn_gain, w_ret_out, att_sink, w_att_out, w_out, norm_ffn2, ffn2_w_in, ffn2_w_out,
           norm_final):
    b, s, d = x.shape
    assert ffn1_w_in.shape[0] == 1, "single-layer block"
    assert all(s % tile == 0 for tile in (TAIL_TILE, FFN_TILE, PROJ_TILE, RET_TILE, ATT_TILE))
    assert TAIL_TILE % TAIL_FFN_ROWS == 0
    xt = x.reshape(b * s, d)
    cosr, sinr, cosa, sina = _rope_tables(s)

    x1, ffn2_w, (w_in_bf, w_ro_bf, w_ao_bf, w_out_bf) = _ffn_call(
        xt, norm_ffn1, ffn1_w_in[0], ffn1_w_out[0], ffn2_w_in[0], ffn2_w_out[0],
        (w_in[0], w_ret_out[0], w_att_out[0], w_out[0]))
    qr, krt, vr, sr, qat, k4, vat, sgr, sga = _proj_call(
        x1, norm_mix, w_in_bf, cosr, sinr, cosa, sina, s)
    dec = jnp.concatenate([ret_decay_fwd, ret_decay_bwd], axis=0)
    zr, oat = _mixer_call(dec, att_sink[0], qr, krt, vr, sr, ret_gn_gain, qat, k4, vat, b, s)
    y = _tail_call(x1, zr, oat, sgr, sga, w_ro_bf, w_ao_bf, w_out_bf, norm_ffn2, *ffn2_w,
                   norm_final.reshape(1, d))
    return y.reshape(b, s, d)
```

```python
import functools

import jax
import jax.numpy as jnp
import numpy as np
from jax import lax
from jax.experimental import pallas as pl
from jax.experimental.pallas import tpu as pltpu

F32 = jnp.float32
BF16 = jnp.bfloat16

EPS = 1e-6
LOG2_E = 1.4426950408889634
ROPE_THETA = 10000.0
RET_HEADS = 4
RET_DK = 128
ATT_Q_HEADS = 8
ATT_KV_HEADS = 2
ATT_HD = 64
ATT_BLOCK = 128

LANES = 128
BF16_SUBLANES = 16
MXU_WIDTH = 256
V7X_VMEM_BYTES = 64 * 1024 * 1024
VMEM_LIMIT = V7X_VMEM_BYTES - 6 * 1024 * 1024

FF_CHUNK = MXU_WIDTH
RET_TILE = 256
ATT_TILE = 1024
ATT_SCORES_AHEAD = 3
FFN_TILE = 1024
STAGE_IN_ROWS = 64
STAGE_OUT_ROWS = 256
PROJ_TILE = 1024
PROJ_RUN_ORDER = (6, 0, 7, 1, 3, 4, 5, 2)
TAIL_TILE = 1024
TAIL_FFN_ROWS = 512


def _cparams(n_grid):
    return pltpu.CompilerParams(dimension_semantics=("arbitrary",) * n_grid,
                                vmem_limit_bytes=VMEM_LIMIT)


def _resident(shape):
    zeros = (0,) * len(shape)
    return pl.BlockSpec(shape, lambda *_: zeros, pipeline_mode=pl.Buffered(1))


def _inv_rms(x):
    return lax.rsqrt(jnp.mean(x * x, axis=-1, keepdims=True) + EPS)


def _rms(x, g):
    return x * _inv_rms(x) * g


def _dot(a, b):
    return jnp.dot(a, b, preferred_element_type=F32)


def _swiglu_residual(x, gn_ref, wg_ref, wu_ref, wo_ref, h_ref, a_ref):
    h_ref[...] = (x * gn_ref[...]).astype(BF16)
    inv_rms = _inv_rms(x)
    fc = FF_CHUNK
    for c in range(wg_ref.shape[1] // fc):
        chunk = slice(c * fc, (c + 1) * fc)
        gu = _dot(h_ref[...], jnp.concatenate([wg_ref[:, chunk], wu_ref[:, chunk]], axis=1)) * inv_rms
        a_ref[:, chunk] = (jax.nn.silu(gu[:, :fc]) * gu[:, fc:]).astype(BF16)
    return x + 0.5 * _dot(a_ref[...], wo_ref[...])


def _stream_rows(w_hbm, stage_ref, sem_ref, slab, consume):
    n_rows = w_hbm.shape[0]
    starts = list(range(0, n_rows, slab))

    def copy(n):
        size = min(slab, n_rows - starts[n])
        return pltpu.make_async_copy(w_hbm.at[pl.ds(starts[n], size), :],
                                     stage_ref.at[n % 2, pl.ds(0, size), :], sem_ref.at[n % 2])

    copy(0).start()
    for n, start in enumerate(starts):
        if n + 1 < len(starts):
            copy(n + 1).start()
        copy(n).wait()
        size = min(slab, n_rows - start)
        consume(slice(start, start + size), stage_ref.at[n % 2, pl.ds(0, size), :])


def _padded_ff(ff):
    return -(-ff // FF_CHUNK) * FF_CHUNK


def _split_gate_up(w, ff):
    ffp = _padded_ff(ff)
    up_from = 2 * ff - ffp
    assert up_from % LANES == 0
    real = lax.broadcasted_iota(jnp.int32, (w.shape[0], ffp), 1) < ff
    gate = jnp.where(real, w[:, :ffp], 0.0).astype(BF16)
    up = pltpu.roll(w[:, up_from:], ffp - (ff - up_from), 1)
    return gate, jnp.where(real, up, 0.0).astype(BF16)


def _stage_ffn_weights(win_hbm, wout_hbm, wg_ref, wu_ref, wo_ref, stage_in, stage_out, sem_in, sem_out):
    ff = wout_hbm.shape[0]
    assert ff % BF16_SUBLANES == 0

    def consume_in(rows, blk):
        wg_ref[rows, :], wu_ref[rows, :] = _split_gate_up(blk[...], ff)

    def consume_out(rows, blk):
        wo_ref[rows, :] = blk[...].astype(BF16)

    _stream_rows(win_hbm, stage_in, sem_in, stage_in.shape[1], consume_in)
    _stream_rows(wout_hbm, stage_out, sem_out, stage_out.shape[1], consume_out)
    wo_ref[ff:, :] = jnp.zeros((wo_ref.shape[0] - ff, wo_ref.shape[1]), BF16)


def _ffn_kernel(x_ref, gn_ref, win_hbm, wout_hbm, win2_ref, wout2_ref, *rest, ff, n_plain):
    plain_in, rest = rest[:n_plain], rest[n_plain:]
    o_ref, wg2_ref, wu2_ref, wo2_ref = rest[:4]
    plain_out, rest = rest[4:4 + n_plain], rest[4 + n_plain:]
    h_ref, a_ref, wg_ref, wu_ref, wo_ref, stage_in, stage_out, sem_in, sem_out = rest

    @pl.when(pl.program_id(0) == 0)
    def _():
        _stage_ffn_weights(win_hbm, wout_hbm, wg_ref, wu_ref, wo_ref, stage_in, stage_out, sem_in, sem_out)

    o_ref[...] = _swiglu_residual(x_ref[...], gn_ref, wg_ref, wu_ref, wo_ref, h_ref, a_ref)

    wg2_ref[...], wu2_ref[...] = _split_gate_up(win2_ref[...], ff)
    slab = wout2_ref.shape[0]
    row = (pl.program_id(0) // 2) * slab + lax.broadcasted_iota(jnp.int32, wout2_ref.shape, 0)
    wo2_ref[...] = jnp.where(row < ff, wout2_ref[...], 0.0).astype(BF16)
    for src, dst in zip(plain_in, plain_out):
        dst[...] = src[...].astype(BF16)


def _ffn_call(x, gn, w_in, w_out, w_in2, w_out2, plain):
    t, d = x.shape
    tm = FFN_TILE
    n = t // tm
    ff = w_out.shape[0]
    ffp = _padded_ff(ff)
    assert n % 2 == 0 and w_in2.shape == w_in.shape and w_out2.shape == w_out.shape
    tok = pl.BlockSpec((tm, d), lambda i: (i, 0))
    hbm = pl.BlockSpec(memory_space=pl.ANY)

    def slab(rows, width):
        assert rows % n == 0 and (rows // n) % BF16_SUBLANES == 0
        return pl.BlockSpec((rows // n, width), lambda i: (i, 0))

    down_rows = ffp // (n // 2)
    assert ffp % (n // 2) == 0 and down_rows % BF16_SUBLANES == 0
    down = pl.BlockSpec((down_rows, d), lambda i: (i // 2, 0))
    plain_specs = [slab(*w.shape) for w in plain]
    outs = pl.pallas_call(
        functools.partial(_ffn_kernel, ff=ff, n_plain=len(plain)),
        grid=(n,),
        in_specs=[tok, _resident(gn.shape), hbm, hbm, slab(d, 2 * ff), down] + plain_specs,
        out_specs=[tok, slab(d, ffp), slab(d, ffp), down] + plain_specs,
        out_shape=[jax.ShapeDtypeStruct((t, d), F32), jax.ShapeDtypeStruct((d, ffp), BF16),
                   jax.ShapeDtypeStruct((d, ffp), BF16), jax.ShapeDtypeStruct((ffp, d), BF16)]
                  + [jax.ShapeDtypeStruct(w.shape, BF16) for w in plain],
        scratch_shapes=[pltpu.VMEM((tm, d), BF16), pltpu.VMEM((tm, ffp), BF16),
                        pltpu.VMEM((d, ffp), BF16), pltpu.VMEM((d, ffp), BF16), pltpu.VMEM((ffp, d), BF16),
                        pltpu.VMEM((2, STAGE_IN_ROWS, 2 * ff), F32), pltpu.VMEM((2, STAGE_OUT_ROWS, d), F32),
                        pltpu.SemaphoreType.DMA((2,)), pltpu.SemaphoreType.DMA((2,))],
        compiler_params=_cparams(1),
        name="ffn1",
    )(x, gn, w_in, w_out, w_in2, w_out2, *plain)
    return outs[0], tuple(outs[1:4]), tuple(outs[4:])


def _rope128(y, cos, sin_signed):
    return y * cos + pltpu.roll(y, 64, 1) * sin_signed


def _rope64(y, cos, sin_signed):
    first_half = (lax.broadcasted_iota(jnp.int32, y.shape, 1) // (ATT_HD // 2)) % 2 == 0
    partner = jnp.where(first_half, pltpu.roll(y, LANES - ATT_HD // 2, 1), pltpu.roll(y, ATT_HD // 2, 1))
    return y * cos + partner * sin_signed


def _proj_kernel(x_ref, gn_ref, w_ref, cosr_ref, sinr_ref, cosa_ref, sina_ref,
                 qr_ref, kr_ref, vr_ref, sr_ref, qa_ref, k4_ref, va_ref, sgr_ref, sga_ref,
                 h_ref, *, ret_w, att_qw):
    h_ref[...] = (x_ref[...] * gn_ref[...]).astype(BF16)
    inv_rms = _inv_rms(x_ref[...])

    def proj(a, b):
        return _dot(h_ref[...], w_ref[:, a:b]) * inv_rms

    cosr, sinr = cosr_ref[...], sinr_ref[...]
    cosa, sina = cosa_ref[...], sina_ref[...]
    d = x_ref.shape[1]
    lane_blocks = [slice(j * LANES, (j + 1) * LANES) for j in range(ret_w // LANES)]
    assert att_qw == ret_w

    def ret_q(c0):
        y = proj(c0, c0 + ret_w)
        for sl in lane_blocks:
            qr_ref[:, sl] = _rope128(y[:, sl], cosr, sinr).astype(BF16)

    def ret_k(c0):
        y = proj(c0, c0 + ret_w)
        for sl in lane_blocks:
            kr_ref[sl, :] = (_rope128(y[:, sl], cosr, sinr) * (RET_DK ** -0.5)).T.astype(BF16)

    def ret_v(c0):
        vr_ref[...] = proj(c0, c0 + ret_w).astype(BF16)

    def ret_g(c0):
        sr_ref[...] = jax.nn.silu(proj(c0, c0 + ret_w)).astype(BF16)

    def att_q(c0):
        y = proj(c0, c0 + att_qw)
        for sl in lane_blocks:
            qa_ref[sl, :] = (_rope64(y[:, sl], cosa, sina) * (ATT_HD ** -0.5 * LOG2_E)).T.astype(BF16)

    def att_kv(c0):
        y = proj(c0, c0 + 2 * LANES)
        kk = _rope64(y[:, :LANES], cosa, sina)
        low = lax.broadcasted_iota(jnp.int32, kk.shape, 1) < ATT_HD
        kswap = pltpu.roll(kk, ATT_HD, 1)
        k4_ref[:, 0 * LANES:1 * LANES] = jnp.where(low, kk, 0.0).astype(BF16)
        k4_ref[:, 1 * LANES:2 * LANES] = jnp.where(low, 0.0, kswap).astype(BF16)
        k4_ref[:, 2 * LANES:3 * LANES] = jnp.where(low, kswap, 0.0).astype(BF16)
        k4_ref[:, 3 * LANES:4 * LANES] = jnp.where(low, 0.0, kk).astype(BF16)
        va_ref[...] = y[:, LANES:].T.astype(BF16)

    def gate_r(c0):
        sgr_ref[...] = jax.nn.sigmoid(proj(c0, c0 + d)).astype(BF16)

    def gate_a(c0):
        sga_ref[...] = jax.nn.sigmoid(proj(c0, c0 + d)).astype(BF16)

    groups = [(ret_q, ret_w), (ret_k, ret_w), (ret_v, ret_w), (ret_g, ret_w), (att_q, att_qw),
              (att_kv, 2 * LANES), (gate_r, d), (gate_a, d)]
    starts = np.cumsum([0] + [width for _, width in groups])
    for n in PROJ_RUN_ORDER:
        groups[n][0](int(starts[n]))


def _proj_call(x1, gn, w, cosr, sinr, cosa, sina, seq):
    t, d = x1.shape
    tm = PROJ_TILE
    ret_w = RET_HEADS * RET_DK
    att_qw = ATT_Q_HEADS * ATT_HD
    n_pos = seq // tm
    tok = lambda width: pl.BlockSpec((tm, width), lambda i: (i, 0))
    tab = pl.BlockSpec((tm, LANES), lambda i: (i % n_pos, 0))
    widths = (ret_w, ret_w, ret_w, ret_w, att_qw, 4 * LANES, LANES, d, d)
    out_specs = [tok(wd) for wd in widths]
    out_shape = [jax.ShapeDtypeStruct((t, wd), BF16) for wd in widths]
    for k_out in (1, 4, 6):
        out_specs[k_out] = pl.BlockSpec((widths[k_out], tm), lambda i: (0, i))
        out_shape[k_out] = jax.ShapeDtypeStruct((widths[k_out], t), BF16)
    return pl.pallas_call(
        functools.partial(_proj_kernel, ret_w=ret_w, att_qw=att_qw),
        grid=(t // tm,),
        in_specs=[tok(d), _resident(gn.shape), _resident(w.shape), tab, tab, tab, tab],
        out_specs=out_specs,
        out_shape=out_shape,
        scratch_shapes=[pltpu.VMEM((tm, d), BF16)],
        compiler_params=_cparams(1),
        name="proj",
    )(x1, gn, w, cosr, sinr, cosa, sina)


def _ret_stages(h, dec_ref, q_ref, kt_ref, v_ref, s_ref, gain_ref, z_ref, kv_ref, st_ref):
    c = RET_TILE
    dk = kt_ref.shape[0]
    n_chunks = q_ref.shape[0] // c

    def log_gamma(direction, shape):
        return -jnp.exp(jnp.full(shape, dec_ref[direction, h], F32))

    def iota(shape, axis):
        return lax.broadcasted_iota(jnp.int32, shape, axis).astype(F32)

    rel = iota((c, c), 0) - iota((c, c), 1)
    decay = jnp.where(rel >= 0, jnp.exp(jnp.maximum(rel, 0.0) * log_gamma(0, (c, c))),
                      jnp.exp(jnp.maximum(-rel, 0.0) * log_gamma(1, (c, c))))
    pos = iota((c, dk), 0)
    kdec_f = jnp.exp((c - 1 - pos) * log_gamma(0, (c, dk)))
    kdec_b = jnp.exp(pos * log_gamma(1, (c, dk)))
    qdec_f = jnp.exp((pos + 1.0) * log_gamma(0, (c, dk)))
    qdec_b = jnp.exp((c - pos) * log_gamma(1, (c, dk)))
    g_f = jnp.exp(c * log_gamma(0, (dk, dk)))
    g_b = jnp.exp(c * log_gamma(1, (dk, dk)))

    def span(i):
        return slice(i * c, (i + 1) * c)

    def kv_stage(i):
        v = v_ref[span(i), :].astype(F32)
        rhs = jnp.concatenate([(v * kdec_f).astype(BF16), (v * kdec_b).astype(BF16)], axis=1)
        kv_ref[i] = _dot(kt_ref[:, span(i)], rhs)

    def state_stage():
        state = jnp.zeros((dk, dk), F32)
        for i in range(n_chunks):
            st_ref[i, :, :dk] = state.astype(BF16)
            state = g_f * state + kv_ref[i, :, :dk]
        state = jnp.zeros((dk, dk), F32)
        for i in reversed(range(n_chunks)):
            st_ref[i, :, dk:] = state.astype(BF16)
            state = g_b * state + kv_ref[i, :, dk:]

    def out_stage(i):
        q = q_ref[span(i), :]
        scores = _dot(q, kt_ref[:, span(i)])
        inner = _dot((scores * decay).astype(BF16), v_ref[span(i), :])
        cross = _dot(q, st_ref[i])
        y = inner + qdec_f * cross[:, :dk] + qdec_b * cross[:, dk:]
        mu = jnp.mean(y, axis=-1, keepdims=True)
        yc = y - mu
        var = jnp.mean(yc * yc, axis=-1, keepdims=True)
        yn = yc * lax.rsqrt(var + EPS) * gain_ref[...]
        z_ref[span(i), :] = (yn * s_ref[span(i), :].astype(F32)).astype(BF16)

    return ([functools.partial(kv_stage, i) for i in range(n_chunks)] + [state_stage]
            + [functools.partial(out_stage, i) for i in range(n_chunks)])


def _att_stages(tile, n_tiles, sink_ref, qt_ref, k_ref, kp_ref, kn_ref, vt_ref, vtp_ref, vtn_ref, ot_ref):
    blk = ATT_BLOCK
    n_blocks = qt_ref.shape[1] // blk
    grp = ATT_Q_HEADS // ATT_KV_HEADS
    hd = ATT_HD
    neg_inf = jnp.float32(-jnp.inf)
    kpos = lax.broadcasted_iota(jnp.int32, (blk, 2 * blk), 0)
    qpos = lax.broadcasted_iota(jnp.int32, (blk, 2 * blk), 1) % blk
    prev_bias = jnp.where(kpos >= qpos, 0.0, neg_inf)
    next_bias = jnp.where(kpos <= qpos, 0.0, neg_inf)
    first_prev_bias = prev_bias + jnp.where(tile > 0, 0.0, neg_inf)
    last_next_bias = next_bias + jnp.where(tile < n_tiles - 1, 0.0, neg_inf)
    left = lax.broadcasted_iota(jnp.int32, (1, 2 * blk), 1) < blk

    def cols(j):
        return slice(j * blk, (j + 1) * blk)

    def scores(j, g, p):
        qt = jnp.concatenate([qt_ref[cols(2 * g), cols(j)], qt_ref[cols(2 * g + 1), cols(j)]], axis=1)
        kv = cols(2 * g + p)
        kwin = jnp.concatenate([kp_ref[:, kv] if j == 0 else k_ref[cols(j - 1), kv], k_ref[cols(j), kv],
                                kn_ref[:, kv] if j == n_blocks - 1 else k_ref[cols(j + 1), kv]], axis=0)
        return _dot(kwin, qt)

    def finish(j, g, p, s):
        s_prev = s[cols(0), :] + (first_prev_bias if j == 0 else prev_bias)
        s_cur = s[cols(1), :]
        s_next = s[cols(2), :] + (last_next_bias if j == n_blocks - 1 else next_bias)
        snk = jnp.where(left, sink_ref[grp * g + p], sink_ref[grp * g + 2 + p]) * LOG2_E
        m = jnp.maximum(jnp.max(jnp.maximum(jnp.maximum(s_prev, s_cur), s_next), axis=0, keepdims=True), snk)
        et = jnp.concatenate([jnp.exp2(s_prev - m).astype(BF16), jnp.exp2(s_cur - m).astype(BF16),
                              jnp.exp2(s_next - m).astype(BF16)], axis=0)
        dims = slice(g * hd, (g + 1) * hd)
        vt = jnp.concatenate([vtp_ref[dims, :] if j == 0 else vt_ref[dims, cols(j - 1)],
                              vt_ref[dims, cols(j)],
                              vtn_ref[dims, :] if j == n_blocks - 1 else vt_ref[dims, cols(j + 1)]], axis=1)
        acc = _dot(jnp.concatenate([vt, jnp.ones((BF16_SUBLANES, 3 * blk), BF16)], axis=0), et)
        denom = acc[hd:hd + 1, :] + jnp.exp2(snk - m)
        ot = acc[:hd, :] * (1.0 / denom)
        head_a = grp * g + p
        head_b = grp * g + 2 + p
        ot_ref[head_a * hd:(head_a + 1) * hd, cols(j)] = ot[:, :blk].astype(BF16)
        ot_ref[head_b * hd:(head_b + 1) * hd, cols(j)] = ot[:, blk:].astype(BF16)

    units = [(j, g, p) for j in range(n_blocks) for g in range(ATT_KV_HEADS) for p in range(2)]
    pending = []

    def stage(n):
        if n == 0:
            pending.extend(scores(*u) for u in units[:ATT_SCORES_AHEAD])
        if n + ATT_SCORES_AHEAD < len(units):
            pending.append(scores(*units[n + ATT_SCORES_AHEAD]))
        finish(*units[n], pending.pop(0))

    return [functools.partial(stage, n) for n in range(len(units))]


def _interleave(a, b):
    if len(a) < len(b):
        a, b = b, a
    out, taken = [], 0
    for n, stage in enumerate(a):
        out.append(stage)
        due = (n + 1) * len(b) // len(a)
        out += b[taken:due]
        taken = due
    return out


def _mixer_kernel(dec_ref, sink_ref, q_ref, kt_ref, v_ref, s_ref, gain_ref,
                  qt_ref, k_ref, kp_ref, kn_ref, vt_ref, vtp_ref, vtn_ref,
                  z_ref, ot_ref, kv_ref, st_ref):
    j = pl.program_id(1)
    ret = _ret_stages(j, dec_ref, q_ref, kt_ref, v_ref, s_ref, gain_ref, z_ref, kv_ref, st_ref)
    att = _att_stages(j, pl.num_programs(1), sink_ref, qt_ref, k_ref, kp_ref, kn_ref,
                      vt_ref, vtp_ref, vtn_ref, ot_ref)
    for stage in _interleave(ret, att):
        stage()


def _mixer_call(dec, sink, qr, krt, vr, sr, gain, qat, k4, vat, batch, seq):
    t, ret_w = qr.shape
    att_w = qat.shape[0]
    tq = ATT_TILE
    n_tiles = seq // tq
    assert n_tiles == RET_HEADS, "one attention tile per retention head in each grid step"
    per_tile = tq // ATT_BLOCK
    n_halo = seq // ATT_BLOCK
    n_chunks = seq // RET_TILE

    def prev_blk(b, i):
        return b * n_halo + jnp.maximum(i * per_tile - 1, 0)

    def next_blk(b, i):
        return b * n_halo + jnp.minimum((i + 1) * per_tile, n_halo - 1)

    def dims_by_tokens(rows):
        return (pl.BlockSpec((rows, tq), lambda b, i: (0, b * n_tiles + i)),
                pl.BlockSpec((rows, ATT_BLOCK), lambda b, i: (0, prev_blk(b, i))),
                pl.BlockSpec((rows, ATT_BLOCK), lambda b, i: (0, next_blk(b, i))))

    kw = k4.shape[1]
    smem = pl.BlockSpec(memory_space=pltpu.SMEM)
    head = pl.BlockSpec((seq, RET_DK), lambda b, h: (b, h))
    return pl.pallas_call(
        _mixer_kernel,
        grid=(batch, n_tiles),
        in_specs=[smem, smem, head, pl.BlockSpec((RET_DK, seq), lambda b, h: (h, b)), head, head,
                  pl.BlockSpec((1, RET_DK), lambda b, h: (0, h)),
                  dims_by_tokens(att_w)[0],
                  pl.BlockSpec((tq, kw), lambda b, i: (b * n_tiles + i, 0)),
                  pl.BlockSpec((ATT_BLOCK, kw), lambda b, i: (prev_blk(b, i), 0)),
                  pl.BlockSpec((ATT_BLOCK, kw), lambda b, i: (next_blk(b, i), 0)),
                  *dims_by_tokens(vat.shape[0])],
        out_specs=[head, dims_by_tokens(att_w)[0]],
        out_shape=[jax.ShapeDtypeStruct((t, ret_w), BF16), jax.ShapeDtypeStruct((att_w, t), BF16)],
        scratch_shapes=[pltpu.VMEM((n_chunks, RET_DK, 2 * RET_DK), F32),
                        pltpu.VMEM((n_chunks, RET_DK, 2 * RET_DK), BF16)],
        compiler_params=_cparams(2),
        name="mixers",
    )(dec, sink, qr, krt, vr, sr, gain, qat, k4, k4, k4, vat, vat, vat)


def _tail_kernel(x_ref, zr_ref, oa_ref, sgr_ref, sga_ref, wro_ref, wao_ref, wout_ref,
                 gn_ref, wg_ref, wu_ref, wo_ref, gfin_ref, y_ref, h_ref, a_ref):
    y_ret = _dot(zr_ref[...], wro_ref[...])
    y_att = lax.dot_general(oa_ref[...], wao_ref[...], (((0,), (0,)), ((), ())), preferred_element_type=F32)
    merged = sgr_ref[...].astype(F32) * y_ret + sga_ref[...].astype(F32) * y_att
    x2 = x_ref[...] + _dot(merged.astype(BF16), wout_ref[...])
    group = h_ref.shape[0]
    for r in range(x2.shape[0] // group):
        rows = slice(r * group, (r + 1) * group)
        x3 = _swiglu_residual(x2[rows, :], gn_ref, wg_ref, wu_ref, wo_ref, h_ref, a_ref)
        y_ref[rows, :] = _rms(x3, gfin_ref[...])


def _tail_call(x1, zr, oa, sgr, sga, wro, wao, wout, gn, wg, wu, wo, gfin):
    t, d = x1.shape
    tm = TAIL_TILE
    ffp = wo.shape[0]
    tok = lambda width: pl.BlockSpec((tm, width), lambda i: (i, 0))
    return pl.pallas_call(
        _tail_kernel,
        grid=(t // tm,),
        in_specs=[tok(d), tok(zr.shape[1]), pl.BlockSpec((oa.shape[0], tm), lambda i: (0, i)), tok(d), tok(d),
                  _resident(wro.shape), _resident(wao.shape), _resident(wout.shape),
                  _resident(gn.shape), _resident(wg.shape), _resident(wu.shape), _resident(wo.shape),
                  _resident(gfin.shape)],
        out_specs=tok(d),
        out_shape=jax.ShapeDtypeStruct((t, d), F32),
        scratch_shapes=[pltpu.VMEM((TAIL_FFN_ROWS, d), BF16), pltpu.VMEM((TAIL_FFN_ROWS, ffp), BF16)],
        compiler_params=_cparams(1),
        name="tail",
    )(x1, zr, oa, sgr, sga, wro, wao, wout, gn, wg, wu, wo, gfin)


def _rope_tables(seq):
    pos = np.arange(seq, dtype=np.float32)[:, None]

    def cs(half):
        inv_freq = np.float32(ROPE_THETA) ** (-np.arange(half, dtype=np.float32) / np.float32(half))
        ang = (pos * inv_freq[None, :].astype(np.float32)).astype(np.float64)
        return np.cos(ang).astype(np.float32), np.sin(ang).astype(np.float32)

    c, s = cs(RET_DK // 2)
    cosr = np.concatenate([c, c], axis=1)
    sinr = np.concatenate([-s, s], axis=1)
    c, s = cs(ATT_HD // 2)
    cosa = np.concatenate([c, c, c, c], axis=1)
    sina = np.concatenate([-s, s, -s, s], axis=1)
    return tuple(jnp.asarray(tab) for tab in (cosr, sinr, cosa, sina))


def kernel(x, norm_ffn1, ffn1_w_in, ffn1_w_out, norm_mix, w_in, ret_decay_fwd, ret_decay_bwd,
           ret_gn_gain, w_ret_out, att_sink, w_att_out, w_out, norm_ffn2, ffn2_w_in, ffn2_w_out,
           norm_final):
    b, s, d = x.shape
    assert ffn1_w_in.shape[0] == 1, "single-layer block"
    assert all(s % tile == 0 for tile in (TAIL_TILE, FFN_TILE, PROJ_TILE, RET_TILE, ATT_TILE))
    assert TAIL_TILE % TAIL_FFN_ROWS == 0
    xt = x.reshape(b * s, d)
    cosr, sinr, cosa, sina = _rope_tables(s)

    x1, ffn2_w, (w_in_bf, w_ro_bf, w_ao_bf, w_out_bf) = _ffn_call(
        xt, norm_ffn1, ffn1_w_in[0], ffn1_w_out[0], ffn2_w_in[0], ffn2_w_out[0],
        (w_in[0], w_ret_out[0], w_att_out[0], w_out[0]))
    qr, krt, vr, sr, qat, k4, vat, sgr, sga = _proj_call(
        x1, norm_mix, w_in_bf, cosr, sinr, cosa, sina, s)
    dec = jnp.concatenate([ret_decay_fwd, ret_decay_bwd], axis=0)
    zr, oat = _mixer_call(dec, att_sink[0], qr, krt, vr, sr, ret_gn_gain, qat, k4, vat, b, s)
    y = _tail_call(x1, zr, oat, sgr, sga, w_ro_bf, w_ao_bf, w_out_bf, norm_ffn2, *ffn2_w,
                   norm_final.reshape(1, d))
    return y.reshape(b, s, d)
```

```python
import functools

import jax
import jax.numpy as jnp
import numpy as np
from jax import lax
from jax.experimental import pallas as pl
from jax.experimental.pallas import tpu as pltpu

F32 = jnp.float32
BF16 = jnp.bfloat16

EPS = 1e-6
LOG2_E = 1.4426950408889634
ROPE_THETA = 10000.0
RET_HEADS = 4
RET_DK = 128
ATT_Q_HEADS = 8
ATT_KV_HEADS = 2
ATT_HD = 64
ATT_BLOCK = 128

LANES = 128
BF16_SUBLANES = 16
MXU_WIDTH = 256
V7X_VMEM_BYTES = 64 * 1024 * 1024
VMEM_LIMIT = V7X_VMEM_BYTES - 6 * 1024 * 1024

FF_CHUNK = MXU_WIDTH
RET_TILE = 256
ATT_TILE = 1024
ATT_SCORES_AHEAD = 3
FFN_TILE = 1024
STAGE_IN_ROWS = 64
STAGE_OUT_ROWS = 256
PROJ_TILE = 1024
PROJ_RUN_ORDER = (6, 0, 7, 1, 3, 4, 5, 2)
TAIL_TILE = 1024
TAIL_FFN_ROWS = 512


def _cparams(n_grid):
    return pltpu.CompilerParams(dimension_semantics=("arbitrary",) * n_grid,
                                vmem_limit_bytes=VMEM_LIMIT)


def _resident(shape):
    zeros = (0,) * len(shape)
    return pl.BlockSpec(shape, lambda *_: zeros, pipeline_mode=pl.Buffered(1))


def _sigmoid(x):
    return 0.5 * jnp.tanh(0.5 * x) + 0.5


def _silu(x):
    return x * _sigmoid(x)


def _inv_rms(x):
    return lax.rsqrt(jnp.mean(x * x, axis=-1, keepdims=True) + EPS)


def _rms(x, g):
    return x * _inv_rms(x) * g


def _dot(a, b):
    return jnp.dot(a, b, preferred_element_type=F32)


def _swiglu_residual(x, gn_ref, wg_ref, wu_ref, wo_ref, h_ref, a_ref):
    h_ref[...] = (x * gn_ref[...]).astype(BF16)
    inv_rms = _inv_rms(x)
    fc = FF_CHUNK
    for c in range(wg_ref.shape[1] // fc):
        chunk = slice(c * fc, (c + 1) * fc)
        gu = _dot(h_ref[...], jnp.concatenate([wg_ref[:, chunk], wu_ref[:, chunk]], axis=1)) * inv_rms
        a_ref[:, chunk] = (_silu(gu[:, :fc]) * gu[:, fc:]).astype(BF16)
    return x + 0.5 * _dot(a_ref[...], wo_ref[...])


def _stream_rows(w_hbm, stage_ref, sem_ref, slab, consume):
    n_rows = w_hbm.shape[0]
    starts = list(range(0, n_rows, slab))

    def copy(n):
        size = min(slab, n_rows - starts[n])
        return pltpu.make_async_copy(w_hbm.at[pl.ds(starts[n], size), :],
                                     stage_ref.at[n % 2, pl.ds(0, size), :], sem_ref.at[n % 2])

    copy(0).start()
    for n, start in enumerate(starts):
        if n + 1 < len(starts):
            copy(n + 1).start()
        copy(n).wait()
        size = min(slab, n_rows - start)
        consume(slice(start, start + size), stage_ref.at[n % 2, pl.ds(0, size), :])


def _padded_ff(ff):
    return -(-ff // FF_CHUNK) * FF_CHUNK


def _split_gate_up(w, ff):
    ffp = _padded_ff(ff)
    up_from = 2 * ff - ffp
    assert up_from % LANES == 0
    real = lax.broadcasted_iota(jnp.int32, (w.shape[0], ffp), 1) < ff
    gate = jnp.where(real, w[:, :ffp], 0.0).astype(BF16)
    up = pltpu.roll(w[:, up_from:], ffp - (ff - up_from), 1)
    return gate, jnp.where(real, up, 0.0).astype(BF16)


def _stage_ffn_weights(win_hbm, wout_hbm, wg_ref, wu_ref, wo_ref, stage_in, stage_out, sem_in, sem_out):
    ff = wout_hbm.shape[0]
    assert ff % BF16_SUBLANES == 0

    def consume_in(rows, blk):
        wg_ref[rows, :], wu_ref[rows, :] = _split_gate_up(blk[...], ff)

    def consume_out(rows, blk):
        wo_ref[rows, :] = blk[...].astype(BF16)

    _stream_rows(win_hbm, stage_in, sem_in, stage_in.shape[1], consume_in)
    _stream_rows(wout_hbm, stage_out, sem_out, stage_out.shape[1], consume_out)
    wo_ref[ff:, :] = jnp.zeros((wo_ref.shape[0] - ff, wo_ref.shape[1]), BF16)


def _ffn_kernel(x_ref, gn_ref, win_hbm, wout_hbm, win2_ref, wout2_ref, *rest, ff, n_plain):
    plain_in, rest = rest[:n_plain], rest[n_plain:]
    o_ref, wg2_ref, wu2_ref, wo2_ref = rest[:4]
    plain_out, rest = rest[4:4 + n_plain], rest[4 + n_plain:]
    h_ref, a_ref, wg_ref, wu_ref, wo_ref, stage_in, stage_out, sem_in, sem_out = rest

    @pl.when(pl.program_id(0) == 0)
    def _():
        _stage_ffn_weights(win_hbm, wout_hbm, wg_ref, wu_ref, wo_ref, stage_in, stage_out, sem_in, sem_out)

    o_ref[...] = _swiglu_residual(x_ref[...], gn_ref, wg_ref, wu_ref, wo_ref, h_ref, a_ref)

    wg2_ref[...], wu2_ref[...] = _split_gate_up(win2_ref[...], ff)
    slab = wout2_ref.shape[0]
    row = (pl.program_id(0) // 2) * slab + lax.broadcasted_iota(jnp.int32, wout2_ref.shape, 0)
    wo2_ref[...] = jnp.where(row < ff, wout2_ref[...], 0.0).astype(BF16)
    for src, dst in zip(plain_in, plain_out):
        dst[...] = src[...].astype(BF16)


def _ffn_call(x, gn, w_in, w_out, w_in2, w_out2, plain):
    t, d = x.shape
    tm = FFN_TILE
    n = t // tm
    ff = w_out.shape[0]
    ffp = _padded_ff(ff)
    assert n % 2 == 0 and w_in2.shape == w_in.shape and w_out2.shape == w_out.shape
    tok = pl.BlockSpec((tm, d), lambda i: (i, 0))
    hbm = pl.BlockSpec(memory_space=pl.ANY)

    def slab(rows, width):
        assert rows % n == 0 and (rows // n) % BF16_SUBLANES == 0
        return pl.BlockSpec((rows // n, width), lambda i: (i, 0))

    down_rows = ffp // (n // 2)
    assert ffp % (n // 2) == 0 and down_rows % BF16_SUBLANES == 0
    down = pl.BlockSpec((down_rows, d), lambda i: (i // 2, 0))
    plain_specs = [slab(*w.shape) for w in plain]
    outs = pl.pallas_call(
        functools.partial(_ffn_kernel, ff=ff, n_plain=len(plain)),
        grid=(n,),
        in_specs=[tok, _resident(gn.shape), hbm, hbm, slab(d, 2 * ff), down] + plain_specs,
        out_specs=[tok, slab(d, ffp), slab(d, ffp), down] + plain_specs,
        out_shape=[jax.ShapeDtypeStruct((t, d), F32), jax.ShapeDtypeStruct((d, ffp), BF16),
                   jax.ShapeDtypeStruct((d, ffp), BF16), jax.ShapeDtypeStruct((ffp, d), BF16)]
                  + [jax.ShapeDtypeStruct(w.shape, BF16) for w in plain],
        scratch_shapes=[pltpu.VMEM((tm, d), BF16), pltpu.VMEM((tm, ffp), BF16),
                        pltpu.VMEM((d, ffp), BF16), pltpu.VMEM((d, ffp), BF16), pltpu.VMEM((ffp, d), BF16),
                        pltpu.VMEM((2, STAGE_IN_ROWS, 2 * ff), F32), pltpu.VMEM((2, STAGE_OUT_ROWS, d), F32),
                        pltpu.SemaphoreType.DMA((2,)), pltpu.SemaphoreType.DMA((2,))],
        compiler_params=_cparams(1),
        name="ffn1",
    )(x, gn, w_in, w_out, w_in2, w_out2, *plain)
    return outs[0], tuple(outs[1:4]), tuple(outs[4:])


def _rope128(y, cos, sin_signed):
    return y * cos + pltpu.roll(y, 64, 1) * sin_signed


def _rope64(y, cos, sin_signed):
    first_half = (lax.broadcasted_iota(jnp.int32, y.shape, 1) // (ATT_HD // 2)) % 2 == 0
    partner = jnp.where(first_half, pltpu.roll(y, LANES - ATT_HD // 2, 1), pltpu.roll(y, ATT_HD // 2, 1))
    return y * cos + partner * sin_signed


def _proj_kernel(x_ref, gn_ref, w_ref, cosr_ref, sinr_ref, cosa_ref, sina_ref,
                 qr_ref, kr_ref, vr_ref, sr_ref, qa_ref, k4_ref, va_ref, sgr_ref, sga_ref,
                 h_ref, *, ret_w, att_qw):
    h_ref[...] = (x_ref[...] * gn_ref[...]).astype(BF16)
    inv_rms = _inv_rms(x_ref[...])

    def proj(a, b):
        return _dot(h_ref[...], w_ref[:, a:b]) * inv_rms

    cosr, sinr = cosr_ref[...], sinr_ref[...]
    cosa, sina = cosa_ref[...], sina_ref[...]
    d = x_ref.shape[1]
    lane_blocks = [slice(j * LANES, (j + 1) * LANES) for j in range(ret_w // LANES)]
    assert att_qw == ret_w

    def ret_q(c0):
        y = proj(c0, c0 + ret_w)
        for sl in lane_blocks:
            qr_ref[:, sl] = _rope128(y[:, sl], cosr, sinr).astype(BF16)

    def ret_k(c0):
        y = proj(c0, c0 + ret_w)
        for sl in lane_blocks:
            kr_ref[sl, :] = (_rope128(y[:, sl], cosr, sinr) * (RET_DK ** -0.5)).T.astype(BF16)

    def ret_v(c0):
        vr_ref[...] = proj(c0, c0 + ret_w).astype(BF16)

    def ret_g(c0):
        sr_ref[...] = _silu(proj(c0, c0 + ret_w)).astype(BF16)

    def att_q(c0):
        y = proj(c0, c0 + att_qw)
        for sl in lane_blocks:
            qa_ref[sl, :] = (_rope64(y[:, sl], cosa, sina) * (ATT_HD ** -0.5 * LOG2_E)).T.astype(BF16)

    def att_kv(c0):
        y = proj(c0, c0 + 2 * LANES)
        kk = _rope64(y[:, :LANES], cosa, sina)
        low = lax.broadcasted_iota(jnp.int32, kk.shape, 1) < ATT_HD
        kswap = pltpu.roll(kk, ATT_HD, 1)
        k4_ref[:, 0 * LANES:1 * LANES] = jnp.where(low, kk, 0.0).astype(BF16)
        k4_ref[:, 1 * LANES:2 * LANES] = jnp.where(low, 0.0, kswap).astype(BF16)
        k4_ref[:, 2 * LANES:3 * LANES] = jnp.where(low, kswap, 0.0).astype(BF16)
        k4_ref[:, 3 * LANES:4 * LANES] = jnp.where(low, 0.0, kk).astype(BF16)
        va_ref[...] = y[:, LANES:].T.astype(BF16)

    def gate_r(c0):
        sgr_ref[...] = _sigmoid(proj(c0, c0 + d)).astype(BF16)

    def gate_a(c0):
        sga_ref[...] = _sigmoid(proj(c0, c0 + d)).astype(BF16)

    groups = [(ret_q, ret_w), (ret_k, ret_w), (ret_v, ret_w), (ret_g, ret_w), (att_q, att_qw),
              (att_kv, 2 * LANES), (gate_r, d), (gate_a, d)]
    starts = np.cumsum([0] + [width for _, width in groups])
    for n in PROJ_RUN_ORDER:
        groups[n][0](int(starts[n]))


def _proj_call(x1, gn, w, cosr, sinr, cosa, sina, seq):
    t, d = x1.shape
    tm = PROJ_TILE
    ret_w = RET_HEADS * RET_DK
    att_qw = ATT_Q_HEADS * ATT_HD
    n_pos = seq // tm
    tok = lambda width: pl.BlockSpec((tm, width), lambda i: (i, 0))
    tab = pl.BlockSpec((tm, LANES), lambda i: (i % n_pos, 0))
    widths = (ret_w, ret_w, ret_w, ret_w, att_qw, 4 * LANES, LANES, d, d)
    out_specs = [tok(wd) for wd in widths]
    out_shape = [jax.ShapeDtypeStruct((t, wd), BF16) for wd in widths]
    for k_out in (1, 4, 6):
        out_specs[k_out] = pl.BlockSpec((widths[k_out], tm), lambda i: (0, i))
        out_shape[k_out] = jax.ShapeDtypeStruct((widths[k_out], t), BF16)
    return pl.pallas_call(
        functools.partial(_proj_kernel, ret_w=ret_w, att_qw=att_qw),
        grid=(t // tm,),
        in_specs=[tok(d), _resident(gn.shape), _resident(w.shape), tab, tab, tab, tab],
        out_specs=out_specs,
        out_shape=out_shape,
        scratch_shapes=[pltpu.VMEM((tm, d), BF16)],
        compiler_params=_cparams(1),
        name="proj",
    )(x1, gn, w, cosr, sinr, cosa, sina)


def _ret_stages(h, dec_ref, q_ref, kt_ref, v_ref, s_ref, gain_ref, z_ref, kv_ref, st_ref):
    c = RET_TILE
    dk = kt_ref.shape[0]
    n_chunks = q_ref.shape[0] // c

    def log_gamma(direction, shape):
        return -jnp.exp(jnp.full(shape, dec_ref[direction, h], F32))

    def iota(shape, axis):
        return lax.broadcasted_iota(jnp.int32, shape, axis).astype(F32)

    rel = iota((c, c), 0) - iota((c, c), 1)
    decay = jnp.where(rel >= 0, jnp.exp(jnp.maximum(rel, 0.0) * log_gamma(0, (c, c))),
                      jnp.exp(jnp.maximum(-rel, 0.0) * log_gamma(1, (c, c))))
    pos = iota((c, dk), 0)
    kdec_f = jnp.exp((c - 1 - pos) * log_gamma(0, (c, dk)))
    kdec_b = jnp.exp(pos * log_gamma(1, (c, dk)))
    qdec_f = jnp.exp((pos + 1.0) * log_gamma(0, (c, dk)))
    qdec_b = jnp.exp((c - pos) * log_gamma(1, (c, dk)))
    g_f = jnp.exp(c * log_gamma(0, (dk, dk)))
    g_b = jnp.exp(c * log_gamma(1, (dk, dk)))

    def span(i):
        return slice(i * c, (i + 1) * c)

    def kv_stage(i):
        v = v_ref[span(i), :].astype(F32)
        rhs = jnp.concatenate([(v * kdec_f).astype(BF16), (v * kdec_b).astype(BF16)], axis=1)
        kv_ref[i] = _dot(kt_ref[:, span(i)], rhs)

    def state_stage():
        state = jnp.zeros((dk, dk), F32)
        for i in range(n_chunks):
            st_ref[i, :, :dk] = state.astype(BF16)
            state = g_f * state + kv_ref[i, :, :dk]
        state = jnp.zeros((dk, dk), F32)
        for i in reversed(range(n_chunks)):
            st_ref[i, :, dk:] = state.astype(BF16)
            state = g_b * state + kv_ref[i, :, dk:]

    def out_stage(i):
        q = q_ref[span(i), :]
        scores = _dot(q, kt_ref[:, span(i)])
        inner = _dot((scores * decay).astype(BF16), v_ref[span(i), :])
        cross = _dot(q, st_ref[i])
        y = inner + qdec_f * cross[:, :dk] + qdec_b * cross[:, dk:]
        mu = jnp.mean(y, axis=-1, keepdims=True)
        yc = y - mu
        var = jnp.mean(yc * yc, axis=-1, keepdims=True)
        yn = yc * lax.rsqrt(var + EPS) * gain_ref[...]
        z_ref[span(i), :] = (yn * s_ref[span(i), :].astype(F32)).astype(BF16)

    return ([functools.partial(kv_stage, i) for i in range(n_chunks)] + [state_stage]
            + [functools.partial(out_stage, i) for i in range(n_chunks)])


def _att_stages(tile, n_tiles, sink_ref, qt_ref, k_ref, kp_ref, kn_ref, vt_ref, vtp_ref, vtn_ref, ot_ref):
    blk = ATT_BLOCK
    n_blocks = qt_ref.shape[1] // blk
    grp = ATT_Q_HEADS // ATT_KV_HEADS
    hd = ATT_HD
    neg_inf = jnp.float32(-jnp.inf)
    kpos = lax.broadcasted_iota(jnp.int32, (blk, 2 * blk), 0)
    qpos = lax.broadcasted_iota(jnp.int32, (blk, 2 * blk), 1) % blk
    prev_bias = jnp.where(kpos >= qpos, 0.0, neg_inf)
    next_bias = jnp.where(kpos <= qpos, 0.0, neg_inf)
    first_prev_bias = prev_bias + jnp.where(tile > 0, 0.0, neg_inf)
    last_next_bias = next_bias + jnp.where(tile < n_tiles - 1, 0.0, neg_inf)
    left = lax.broadcasted_iota(jnp.int32, (1, 2 * blk), 1) < blk

    def cols(j):
        return slice(j * blk, (j + 1) * blk)

    def scores(j, g, p):
        qt = jnp.concatenate([qt_ref[cols(2 * g), cols(j)], qt_ref[cols(2 * g + 1), cols(j)]], axis=1)
        kv = cols(2 * g + p)
        kwin = jnp.concatenate([kp_ref[:, kv] if j == 0 else k_ref[cols(j - 1), kv], k_ref[cols(j), kv],
                                kn_ref[:, kv] if j == n_blocks - 1 else k_ref[cols(j + 1), kv]], axis=0)
        return _dot(kwin, qt)

    def finish(j, g, p, s):
        s_prev = s[cols(0), :] + (first_prev_bias if j == 0 else prev_bias)
        s_cur = s[cols(1), :]
        s_next = s[cols(2), :] + (last_next_bias if j == n_blocks - 1 else next_bias)
        snk = jnp.where(left, sink_ref[grp * g + p], sink_ref[grp * g + 2 + p]) * LOG2_E
        m = jnp.maximum(jnp.max(jnp.maximum(jnp.maximum(s_prev, s_cur), s_next), axis=0, keepdims=True), snk)
        et = jnp.concatenate([jnp.exp2(s_prev - m).astype(BF16), jnp.exp2(s_cur - m).astype(BF16),
                              jnp.exp2(s_next - m).astype(BF16)], axis=0)
        dims = slice(g * hd, (g + 1) * hd)
        vt = jnp.concatenate([vtp_ref[dims, :] if j == 0 else vt_ref[dims, cols(j - 1)],
                              vt_ref[dims, cols(j)],
                              vtn_ref[dims, :] if j == n_blocks - 1 else vt_ref[dims, cols(j + 1)]], axis=1)
        acc = _dot(jnp.concatenate([vt, jnp.ones((BF16_SUBLANES, 3 * blk), BF16)], axis=0), et)
        denom = acc[hd:hd + 1, :] + jnp.exp2(snk - m)
        ot = acc[:hd, :] * (1.0 / denom)
        head_a = grp * g + p
        head_b = grp * g + 2 + p
        ot_ref[head_a * hd:(head_a + 1) * hd, cols(j)] = ot[:, :blk].astype(BF16)
        ot_ref[head_b * hd:(head_b + 1) * hd, cols(j)] = ot[:, blk:].astype(BF16)

    units = [(j, g, p) for j in range(n_blocks) for g in range(ATT_KV_HEADS) for p in range(2)]
    pending = []

    def stage(n):
        if n == 0:
            pending.extend(scores(*u) for u in units[:ATT_SCORES_AHEAD])
        if n + ATT_SCORES_AHEAD < len(units):
            pending.append(scores(*units[n + ATT_SCORES_AHEAD]))
        finish(*units[n], pending.pop(0))

    return [functools.partial(stage, n) for n in range(len(units))]


def _interleave(a, b):
    if len(a) < len(b):
        a, b = b, a
    out, taken = [], 0
    for n, stage in enumerate(a):
        out.append(stage)
        due = (n + 1) * len(b) // len(a)
        out += b[taken:due]
        taken = due
    return out


def _mixer_kernel(dec_ref, sink_ref, q_ref, kt_ref, v_ref, s_ref, gain_ref,
                  qt_ref, k_ref, kp_ref, kn_ref, vt_ref, vtp_ref, vtn_ref,
                  z_ref, ot_ref, kv_ref, st_ref):
    j = pl.program_id(1)
    ret = _ret_stages(j, dec_ref, q_ref, kt_ref, v_ref, s_ref, gain_ref, z_ref, kv_ref, st_ref)
    att = _att_stages(j, pl.num_programs(1), sink_ref, qt_ref, k_ref, kp_ref, kn_ref,
                      vt_ref, vtp_ref, vtn_ref, ot_ref)
    for stage in _interleave(ret, att):
        stage()


def _mixer_call(dec, sink, qr, krt, vr, sr, gain, qat, k4, vat, batch, seq):
    t, ret_w = qr.shape
    att_w = qat.shape[0]
    tq = ATT_TILE
    n_tiles = seq // tq
    assert n_tiles == RET_HEADS, "one attention tile per retention head in each grid step"
    per_tile = tq // ATT_BLOCK
    n_halo = seq // ATT_BLOCK
    n_chunks = seq // RET_TILE

    def prev_blk(b, i):
        return b * n_halo + jnp.maximum(i * per_tile - 1, 0)

    def next_blk(b, i):
        return b * n_halo + jnp.minimum((i + 1) * per_tile, n_halo - 1)

    def dims_by_tokens(rows):
        return (pl.BlockSpec((rows, tq), lambda b, i: (0, b * n_tiles + i)),
                pl.BlockSpec((rows, ATT_BLOCK), lambda b, i: (0, prev_blk(b, i))),
                pl.BlockSpec((rows, ATT_BLOCK), lambda b, i: (0, next_blk(b, i))))

    kw = k4.shape[1]
    smem = pl.BlockSpec(memory_space=pltpu.SMEM)
    head = pl.BlockSpec((seq, RET_DK), lambda b, h: (b, h))
    return pl.pallas_call(
        _mixer_kernel,
        grid=(batch, n_tiles),
        in_specs=[smem, smem, head, pl.BlockSpec((RET_DK, seq), lambda b, h: (h, b)), head, head,
                  pl.BlockSpec((1, RET_DK), lambda b, h: (0, h)),
                  dims_by_tokens(att_w)[0],
                  pl.BlockSpec((tq, kw), lambda b, i: (b * n_tiles + i, 0)),
                  pl.BlockSpec((ATT_BLOCK, kw), lambda b, i: (prev_blk(b, i), 0)),
                  pl.BlockSpec((ATT_BLOCK, kw), lambda b, i: (next_blk(b, i), 0)),
                  *dims_by_tokens(vat.shape[0])],
        out_specs=[head, dims_by_tokens(att_w)[0]],
        out_shape=[jax.ShapeDtypeStruct((t, ret_w), BF16), jax.ShapeDtypeStruct((att_w, t), BF16)],
        scratch_shapes=[pltpu.VMEM((n_chunks, RET_DK, 2 * RET_DK), F32),
                        pltpu.VMEM((n_chunks, RET_DK, 2 * RET_DK), BF16)],
        compiler_params=_cparams(2),
        name="mixers",
    )(dec, sink, qr, krt, vr, sr, gain, qat, k4, k4, k4, vat, vat, vat)


def _tail_kernel(x_ref, zr_ref, oa_ref, sgr_ref, sga_ref, wro_ref, wao_ref, wout_ref,
                 gn_ref, wg_ref, wu_ref, wo_ref, gfin_ref, y_ref, h_ref, a_ref):
    y_ret = _dot(zr_ref[...], wro_ref[...])
    y_att = lax.dot_general(oa_ref[...], wao_ref[...], (((0,), (0,)), ((), ())), preferred_element_type=F32)
    merged = sgr_ref[...].astype(F32) * y_ret + sga_ref[...].astype(F32) * y_att
    x2 = x_ref[...] + _dot(merged.astype(BF16), wout_ref[...])
    group = h_ref.shape[0]
    for r in range(x2.shape[0] // group):
        rows = slice(r * group, (r + 1) * group)
        x3 = _swiglu_residual(x2[rows, :], gn_ref, wg_ref, wu_ref, wo_ref, h_ref, a_ref)
        y_ref[rows, :] = _rms(x3, gfin_ref[...])


def _tail_call(x1, zr, oa, sgr, sga, wro, wao, wout, gn, wg, wu, wo, gfin):
    t, d = x1.shape
    tm = TAIL_TILE
    ffp = wo.shape[0]
    tok = lambda width: pl.BlockSpec((tm, width), lambda i: (i, 0))
    return pl.pallas_call(
        _tail_kernel,
        grid=(t // tm,),
        in_specs=[tok(d), tok(zr.shape[1]), pl.BlockSpec((oa.shape[0], tm), lambda i: (0, i)), tok(d), tok(d),
                  _resident(wro.shape), _resident(wao.shape), _resident(wout.shape),
                  _resident(gn.shape), _resident(wg.shape), _resident(wu.shape), _resident(wo.shape),
                  _resident(gfin.shape)],
        out_specs=tok(d),
        out_shape=jax.ShapeDtypeStruct((t, d), F32),
        scratch_shapes=[pltpu.VMEM((TAIL_FFN_ROWS, d), BF16), pltpu.VMEM((TAIL_FFN_ROWS, ffp), BF16)],
        compiler_params=_cparams(1),
        name="tail",
    )(x1, zr, oa, sgr, sga, wro, wao, wout, gn, wg, wu, wo, gfin)


def _rope_tables(seq):
    pos = np.arange(seq, dtype=np.float32)[:, None]

    def cs(half):
        inv_freq = np.float32(ROPE_THETA) ** (-np.arange(half, dtype=np.float32) / np.float32(half))
        ang = (pos * inv_freq[None, :].astype(np.float32)).astype(np.float64)
        return np.cos(ang).astype(np.float32), np.sin(ang).astype(np.float32)

    c, s = cs(RET_DK // 2)
    cosr = np.concatenate([c, c], axis=1)
    sinr = np.concatenate([-s, s], axis=1)
    c, s = cs(ATT_HD // 2)
    cosa = np.concatenate([c, c, c, c], axis=1)
    sina = np.concatenate([-s, s, -s, s], axis=1)
    return tuple(jnp.asarray(tab) for tab in (cosr, sinr, cosa, sina))


def kernel(x, norm_ffn1, ffn1_w_in, ffn1_w_out, norm_mix, w_in, ret_decay_fwd, ret_decay_bwd,
           ret_gn_gain, w_ret_out, att_sink, w_att_out, w_out, norm_ffn2, ffn2_w_in, ffn2_w_out,
           norm_final):
    b, s, d = x.shape
    assert ffn1_w_in.shape[0] == 1, "single-layer block"
    assert all(s % tile == 0 for tile in (TAIL_TILE, FFN_TILE, PROJ_TILE, RET_TILE, ATT_TILE))
    assert TAIL_TILE % TAIL_FFN_ROWS == 0
    xt = x.reshape(b * s, d)
    cosr, sinr, cosa, sina = _rope_tables(s)

    x1, ffn2_w, (w_in_bf, w_ro_bf, w_ao_bf, w_out_bf) = _ffn_call(
        xt, norm_ffn1, ffn1_w_in[0], ffn1_w_out[0], ffn2_w_in[0], ffn2_w_out[0],
        (w_in[0], w_ret_out[0], w_att_out[0], w_out[0]))
    qr, krt, vr, sr, qat, k4, vat, sgr, sga = _proj_call(
        x1, norm_mix, w_in_bf, cosr, sinr, cosa, sina, s)
    dec = jnp.concatenate([ret_decay_fwd, ret_decay_bwd], axis=0)
    zr, oat = _mixer_call(dec, att_sink[0], qr, krt, vr, sr, ret_gn_gain, qat, k4, vat, b, s)
    y = _tail_call(x1, zr, oat, sgr, sga, w_ro_bf, w_ao_bf, w_out_bf, norm_ffn2, *ffn2_w,
                   norm_final.reshape(1, d))
    return y.reshape(b, s, d)
```

```python
import functools

import jax
import jax.numpy as jnp
import numpy as np
from jax import lax
from jax.experimental import pallas as pl
from jax.experimental.pallas import tpu as pltpu

F32 = jnp.float32
BF16 = jnp.bfloat16

EPS = 1e-6
LOG2_E = 1.4426950408889634
ROPE_THETA = 10000.0
RET_HEADS = 4
RET_DK = 128
ATT_Q_HEADS = 8
ATT_KV_HEADS = 2
ATT_HD = 64
ATT_BLOCK = 128

LANES = 128
BF16_SUBLANES = 16
MXU_WIDTH = 256
V7X_VMEM_BYTES = 64 * 1024 * 1024
VMEM_LIMIT = V7X_VMEM_BYTES - 6 * 1024 * 1024

FF_CHUNK = MXU_WIDTH
RET_TILE = 256
ATT_TILE = 1024
ATT_SCORES_AHEAD = 3
FFN_TILE = 1024
STAGE_IN_ROWS = 128
STAGE_OUT_ROWS = 512
PROJ_TILE = 1024
PROJ_RUN_ORDER = (6, 0, 7, 1, 3, 4, 5, 2)
TAIL_TILE = 1024
TAIL_FFN_ROWS = 512


def _cparams(n_grid):
    return pltpu.CompilerParams(dimension_semantics=("arbitrary",) * n_grid,
                                vmem_limit_bytes=VMEM_LIMIT)


def _resident(shape):
    zeros = (0,) * len(shape)
    return pl.BlockSpec(shape, lambda *_: zeros, pipeline_mode=pl.Buffered(1))


def _sigmoid(x):
    return 0.5 * jnp.tanh(0.5 * x) + 0.5


def _silu(x):
    return x * _sigmoid(x)


def _inv_rms(x):
    return lax.rsqrt(jnp.mean(x * x, axis=-1, keepdims=True) + EPS)


def _rms(x, g):
    return x * _inv_rms(x) * g


def _dot(a, b):
    return jnp.dot(a, b, preferred_element_type=F32)


def _swiglu_residual(x, gn_ref, wg_ref, wu_ref, wo_ref, h_ref, a_ref):
    h_ref[...] = (x * gn_ref[...]).astype(BF16)
    inv_rms = _inv_rms(x)
    fc = FF_CHUNK
    for c in range(wg_ref.shape[1] // fc):
        chunk = slice(c * fc, (c + 1) * fc)
        gu = _dot(h_ref[...], jnp.concatenate([wg_ref[:, chunk], wu_ref[:, chunk]], axis=1)) * inv_rms
        a_ref[:, chunk] = (_silu(gu[:, :fc]) * gu[:, fc:]).astype(BF16)
    return x + 0.5 * _dot(a_ref[...], wo_ref[...])


def _stream_rows(w_hbm, stage_ref, sem_ref, slab, consume):
    n_rows = w_hbm.shape[0]
    starts = list(range(0, n_rows, slab))

    def copy(n):
        size = min(slab, n_rows - starts[n])
        return pltpu.make_async_copy(w_hbm.at[pl.ds(starts[n], size), :],
                                     stage_ref.at[n % 2, pl.ds(0, size), :], sem_ref.at[n % 2])

    copy(0).start()
    for n, start in enumerate(starts):
        if n + 1 < len(starts):
            copy(n + 1).start()
        copy(n).wait()
        size = min(slab, n_rows - start)
        consume(slice(start, start + size), stage_ref.at[n % 2, pl.ds(0, size), :])


def _padded_ff(ff):
    return -(-ff // FF_CHUNK) * FF_CHUNK


def _split_gate_up(w, ff):
    ffp = _padded_ff(ff)
    up_from = 2 * ff - ffp
    assert up_from % LANES == 0
    real = lax.broadcasted_iota(jnp.int32, (w.shape[0], ffp), 1) < ff
    gate = jnp.where(real, w[:, :ffp], 0.0).astype(BF16)
    up = pltpu.roll(w[:, up_from:], ffp - (ff - up_from), 1)
    return gate, jnp.where(real, up, 0.0).astype(BF16)


def _stage_ffn_weights(win_hbm, wout_hbm, wg_ref, wu_ref, wo_ref, stage_in, stage_out, sem_in, sem_out):
    ff = wout_hbm.shape[0]
    assert ff % BF16_SUBLANES == 0

    def consume_in(rows, blk):
        wg_ref[rows, :], wu_ref[rows, :] = _split_gate_up(blk[...], ff)

    def consume_out(rows, blk):
        wo_ref[rows, :] = blk[...].astype(BF16)

    _stream_rows(win_hbm, stage_in, sem_in, stage_in.shape[1], consume_in)
    _stream_rows(wout_hbm, stage_out, sem_out, stage_out.shape[1], consume_out)
    wo_ref[ff:, :] = jnp.zeros((wo_ref.shape[0] - ff, wo_ref.shape[1]), BF16)


def _ffn_kernel(x_ref, gn_ref, win_hbm, wout_hbm, win2_ref, wout2_ref, *rest, ff, n_plain):
    plain_in, rest = rest[:n_plain], rest[n_plain:]
    o_ref, wg2_ref, wu2_ref, wo2_ref = rest[:4]
    plain_out, rest = rest[4:4 + n_plain], rest[4 + n_plain:]
    h_ref, a_ref, wg_ref, wu_ref, wo_ref, stage_in, stage_out, sem_in, sem_out = rest

    @pl.when(pl.program_id(0) == 0)
    def _():
        _stage_ffn_weights(win_hbm, wout_hbm, wg_ref, wu_ref, wo_ref, stage_in, stage_out, sem_in, sem_out)

    o_ref[...] = _swiglu_residual(x_ref[...], gn_ref, wg_ref, wu_ref, wo_ref, h_ref, a_ref)

    wg2_ref[...], wu2_ref[...] = _split_gate_up(win2_ref[...], ff)
    slab = wout2_ref.shape[0]
    row = (pl.program_id(0) // 2) * slab + lax.broadcasted_iota(jnp.int32, wout2_ref.shape, 0)
    wo2_ref[...] = jnp.where(row < ff, wout2_ref[...], 0.0).astype(BF16)
    for src, dst in zip(plain_in, plain_out):
        dst[...] = src[...].astype(BF16)


def _ffn_call(x, gn, w_in, w_out, w_in2, w_out2, plain):
    t, d = x.shape
    tm = FFN_TILE
    n = t // tm
    ff = w_out.shape[0]
    ffp = _padded_ff(ff)
    assert n % 2 == 0 and w_in2.shape == w_in.shape and w_out2.shape == w_out.shape
    tok = pl.BlockSpec((tm, d), lambda i: (i, 0))
    hbm = pl.BlockSpec(memory_space=pl.ANY)

    def slab(rows, width):
        assert rows % n == 0 and (rows // n) % BF16_SUBLANES == 0
        return pl.BlockSpec((rows // n, width), lambda i: (i, 0))

    down_rows = ffp // (n // 2)
    assert ffp % (n // 2) == 0 and down_rows % BF16_SUBLANES == 0
    down = pl.BlockSpec((down_rows, d), lambda i: (i // 2, 0))
    plain_specs = [slab(*w.shape) for w in plain]
    outs = pl.pallas_call(
        functools.partial(_ffn_kernel, ff=ff, n_plain=len(plain)),
        grid=(n,),
        in_specs=[tok, _resident(gn.shape), hbm, hbm, slab(d, 2 * ff), down] + plain_specs,
        out_specs=[tok, slab(d, ffp), slab(d, ffp), down] + plain_specs,
        out_shape=[jax.ShapeDtypeStruct((t, d), F32), jax.ShapeDtypeStruct((d, ffp), BF16),
                   jax.ShapeDtypeStruct((d, ffp), BF16), jax.ShapeDtypeStruct((ffp, d), BF16)]
                  + [jax.ShapeDtypeStruct(w.shape, BF16) for w in plain],
        scratch_shapes=[pltpu.VMEM((tm, d), BF16), pltpu.VMEM((tm, ffp), BF16),
                        pltpu.VMEM((d, ffp), BF16), pltpu.VMEM((d, ffp), BF16), pltpu.VMEM((ffp, d), BF16),
                        pltpu.VMEM((2, STAGE_IN_ROWS, 2 * ff), F32), pltpu.VMEM((2, STAGE_OUT_ROWS, d), F32),
                        pltpu.SemaphoreType.DMA((2,)), pltpu.SemaphoreType.DMA((2,))],
        compiler_params=_cparams(1),
        name="ffn1",
    )(x, gn, w_in, w_out, w_in2, w_out2, *plain)
    return outs[0], tuple(outs[1:4]), tuple(outs[4:])


def _rope128(y, cos, sin_signed):
    return y * cos + pltpu.roll(y, RET_DK // 2, 1) * sin_signed


def _rope64(y, cos, sin_signed):
    first_half = (lax.broadcasted_iota(jnp.int32, y.shape, 1) // (ATT_HD // 2)) % 2 == 0
    partner = jnp.where(first_half, pltpu.roll(y, LANES - ATT_HD // 2, 1), pltpu.roll(y, ATT_HD // 2, 1))
    return y * cos + partner * sin_signed


def _proj_kernel(x_ref, gn_ref, w_ref, cosr_ref, sinr_ref, cosa_ref, sina_ref,
                 qr_ref, kr_ref, vr_ref, sr_ref, qa_ref, k4_ref, va_ref, sgr_ref, sga_ref,
                 h_ref, *, ret_w, att_qw):
    h_ref[...] = (x_ref[...] * gn_ref[...]).astype(BF16)
    inv_rms = _inv_rms(x_ref[...])

    def proj(a, b):
        return _dot(h_ref[...], w_ref[:, a:b]) * inv_rms

    cosr, sinr = cosr_ref[...], sinr_ref[...]
    cosa, sina = cosa_ref[...], sina_ref[...]
    d = x_ref.shape[1]
    lane_blocks = [slice(j * LANES, (j + 1) * LANES) for j in range(ret_w // LANES)]
    assert att_qw == ret_w

    def ret_q(c0):
        y = proj(c0, c0 + ret_w)
        for sl in lane_blocks:
            qr_ref[:, sl] = _rope128(y[:, sl], cosr, sinr).astype(BF16)

    def ret_k(c0):
        y = proj(c0, c0 + ret_w)
        for sl in lane_blocks:
            kr_ref[sl, :] = (_rope128(y[:, sl], cosr, sinr) * (RET_DK ** -0.5)).T.astype(BF16)

    def ret_v(c0):
        vr_ref[...] = proj(c0, c0 + ret_w).astype(BF16)

    def ret_g(c0):
        sr_ref[...] = _silu(proj(c0, c0 + ret_w)).astype(BF16)

    def att_q(c0):
        y = proj(c0, c0 + att_qw)
        for sl in lane_blocks:
            qa_ref[sl, :] = (_rope64(y[:, sl], cosa, sina) * (ATT_HD ** -0.5 * LOG2_E)).T.astype(BF16)

    def att_kv(c0):
        y = proj(c0, c0 + 2 * LANES)
        kk = _rope64(y[:, :LANES], cosa, sina)
        low = lax.broadcasted_iota(jnp.int32, kk.shape, 1) < ATT_HD
        kswap = pltpu.roll(kk, ATT_HD, 1)
        k4_ref[:, 0 * LANES:1 * LANES] = jnp.where(low, kk, 0.0).astype(BF16)
        k4_ref[:, 1 * LANES:2 * LANES] = jnp.where(low, 0.0, kswap).astype(BF16)
        k4_ref[:, 2 * LANES:3 * LANES] = jnp.where(low, kswap, 0.0).astype(BF16)
        k4_ref[:, 3 * LANES:4 * LANES] = jnp.where(low, 0.0, kk).astype(BF16)
        va_ref[...] = y[:, LANES:].T.astype(BF16)

    def gate_r(c0):
        sgr_ref[...] = _sigmoid(proj(c0, c0 + d)).astype(BF16)

    def gate_a(c0):
        sga_ref[...] = _sigmoid(proj(c0, c0 + d)).astype(BF16)

    groups = [(ret_q, ret_w), (ret_k, ret_w), (ret_v, ret_w), (ret_g, ret_w), (att_q, att_qw),
              (att_kv, 2 * LANES), (gate_r, d), (gate_a, d)]
    starts = np.cumsum([0] + [width for _, width in groups])
    for n in PROJ_RUN_ORDER:
        groups[n][0](int(starts[n]))


def _proj_call(x1, gn, w, cosr, sinr, cosa, sina, seq):
    t, d = x1.shape
    tm = PROJ_TILE
    ret_w = RET_HEADS * RET_DK
    att_qw = ATT_Q_HEADS * ATT_HD
    n_pos = seq // tm
    tok = lambda width: pl.BlockSpec((tm, width), lambda i: (i, 0))
    tab = pl.BlockSpec((tm, LANES), lambda i: (i % n_pos, 0))
    widths = (ret_w, ret_w, ret_w, ret_w, att_qw, 4 * LANES, LANES, d, d)
    out_specs = [tok(wd) for wd in widths]
    out_shape = [jax.ShapeDtypeStruct((t, wd), BF16) for wd in widths]
    for k_out in (1, 4, 6):
        out_specs[k_out] = pl.BlockSpec((widths[k_out], tm), lambda i: (0, i))
        out_shape[k_out] = jax.ShapeDtypeStruct((widths[k_out], t), BF16)
    return pl.pallas_call(
        functools.partial(_proj_kernel, ret_w=ret_w, att_qw=att_qw),
        grid=(t // tm,),
        in_specs=[tok(d), _resident(gn.shape), _resident(w.shape), tab, tab, tab, tab],
        out_specs=out_specs,
        out_shape=out_shape,
        scratch_shapes=[pltpu.VMEM((tm, d), BF16)],
        compiler_params=_cparams(1),
        name="proj",
    )(x1, gn, w, cosr, sinr, cosa, sina)


def _ret_stages(h, dec_ref, q_ref, kt_ref, v_ref, s_ref, gain_ref, z_ref, kv_ref, st_ref):
    c = RET_TILE
    dk = kt_ref.shape[0]
    n_chunks = q_ref.shape[0] // c

    def log_gamma(direction, shape):
        return -jnp.exp(jnp.full(shape, dec_ref[direction, h], F32))

    def iota(shape, axis):
        return lax.broadcasted_iota(jnp.int32, shape, axis).astype(F32)

    rel = iota((c, c), 0) - iota((c, c), 1)
    decay = jnp.where(rel >= 0, jnp.exp(jnp.maximum(rel, 0.0) * log_gamma(0, (c, c))),
                      jnp.exp(jnp.maximum(-rel, 0.0) * log_gamma(1, (c, c))))
    pos = iota((c, dk), 0)
    kdec_f = jnp.exp((c - 1 - pos) * log_gamma(0, (c, dk)))
    kdec_b = jnp.exp(pos * log_gamma(1, (c, dk)))
    qdec_f = jnp.exp((pos + 1.0) * log_gamma(0, (c, dk)))
    qdec_b = jnp.exp((c - pos) * log_gamma(1, (c, dk)))
    g_f = jnp.exp(c * log_gamma(0, (dk, dk)))
    g_b = jnp.exp(c * log_gamma(1, (dk, dk)))

    def span(i):
        return slice(i * c, (i + 1) * c)

    def kv_stage(i):
        v = v_ref[span(i), :].astype(F32)
        rhs = jnp.concatenate([(v * kdec_f).astype(BF16), (v * kdec_b).astype(BF16)], axis=1)
        kv_ref[i] = _dot(kt_ref[:, span(i)], rhs)

    def state_stage():
        state = jnp.zeros((dk, dk), F32)
        for i in range(n_chunks):
            st_ref[i, :, :dk] = state.astype(BF16)
            state = g_f * state + kv_ref[i, :, :dk]
        state = jnp.zeros((dk, dk), F32)
        for i in reversed(range(n_chunks)):
            st_ref[i, :, dk:] = state.astype(BF16)
            state = g_b * state + kv_ref[i, :, dk:]

    def out_stage(i):
        q = q_ref[span(i), :]
        scores = _dot(q, kt_ref[:, span(i)])
        inner = _dot((scores * decay).astype(BF16), v_ref[span(i), :])
        cross = _dot(q, st_ref[i])
        y = inner + qdec_f * cross[:, :dk] + qdec_b * cross[:, dk:]
        mu = jnp.mean(y, axis=-1, keepdims=True)
        yc = y - mu
        var = jnp.mean(yc * yc, axis=-1, keepdims=True)
        yn = yc * lax.rsqrt(var + EPS) * gain_ref[...]
        z_ref[span(i), :] = (yn * s_ref[span(i), :].astype(F32)).astype(BF16)

    return ([functools.partial(kv_stage, i) for i in range(n_chunks)] + [state_stage]
            + [functools.partial(out_stage, i) for i in range(n_chunks)])


def _att_stages(tile, n_tiles, sink_ref, qt_ref, k_ref, kp_ref, kn_ref, vt_ref, vtp_ref, vtn_ref, ot_ref):
    blk = ATT_BLOCK
    n_blocks = qt_ref.shape[1] // blk
    grp = ATT_Q_HEADS // ATT_KV_HEADS
    hd = ATT_HD
    neg_inf = jnp.float32(-jnp.inf)
    kpos = lax.broadcasted_iota(jnp.int32, (blk, 2 * blk), 0)
    qpos = lax.broadcasted_iota(jnp.int32, (blk, 2 * blk), 1) % blk
    prev_bias = jnp.where(kpos >= qpos, 0.0, neg_inf)
    next_bias = jnp.where(kpos <= qpos, 0.0, neg_inf)
    first_prev_bias = prev_bias + jnp.where(tile > 0, 0.0, neg_inf)
    last_next_bias = next_bias + jnp.where(tile < n_tiles - 1, 0.0, neg_inf)
    left = lax.broadcasted_iota(jnp.int32, (1, 2 * blk), 1) < blk

    def cols(j):
        return slice(j * blk, (j + 1) * blk)

    def scores(j, g, p):
        qt = jnp.concatenate([qt_ref[cols(2 * g), cols(j)], qt_ref[cols(2 * g + 1), cols(j)]], axis=1)
        kv = cols(2 * g + p)
        kwin = jnp.concatenate([kp_ref[:, kv] if j == 0 else k_ref[cols(j - 1), kv], k_ref[cols(j), kv],
                                kn_ref[:, kv] if j == n_blocks - 1 else k_ref[cols(j + 1), kv]], axis=0)
        return _dot(kwin, qt)

    def finish(j, g, p, s):
        s_prev = s[cols(0), :] + (first_prev_bias if j == 0 else prev_bias)
        s_cur = s[cols(1), :]
        s_next = s[cols(2), :] + (last_next_bias if j == n_blocks - 1 else next_bias)
        snk = jnp.where(left, sink_ref[grp * g + p], sink_ref[grp * g + 2 + p]) * LOG2_E
        m = jnp.maximum(jnp.max(jnp.maximum(jnp.maximum(s_prev, s_cur), s_next), axis=0, keepdims=True), snk)
        et = jnp.concatenate([jnp.exp2(s_prev - m).astype(BF16), jnp.exp2(s_cur - m).astype(BF16),
                              jnp.exp2(s_next - m).astype(BF16)], axis=0)
        dims = slice(g * hd, (g + 1) * hd)
        vt = jnp.concatenate([vtp_ref[dims, :] if j == 0 else vt_ref[dims, cols(j - 1)],
                              vt_ref[dims, cols(j)],
                              vtn_ref[dims, :] if j == n_blocks - 1 else vt_ref[dims, cols(j + 1)]], axis=1)
        acc = _dot(jnp.concatenate([vt, jnp.ones((BF16_SUBLANES, 3 * blk), BF16)], axis=0), et)
        denom = acc[hd:hd + 1, :] + jnp.exp2(snk - m)
        ot = acc[:hd, :] * (1.0 / denom)
        head_a = grp * g + p
        head_b = grp * g + 2 + p
        ot_ref[head_a * hd:(head_a + 1) * hd, cols(j)] = ot[:, :blk].astype(BF16)
        ot_ref[head_b * hd:(head_b + 1) * hd, cols(j)] = ot[:, blk:].astype(BF16)

    units = [(j, g, p) for j in range(n_blocks) for g in range(ATT_KV_HEADS) for p in range(2)]
    pending = []

    def stage(n):
        if n == 0:
            pending.extend(scores(*u) for u in units[:ATT_SCORES_AHEAD])
        if n + ATT_SCORES_AHEAD < len(units):
            pending.append(scores(*units[n + ATT_SCORES_AHEAD]))
        finish(*units[n], pending.pop(0))

    return [functools.partial(stage, n) for n in range(len(units))]


def _interleave(a, b):
    if len(a) < len(b):
        a, b = b, a
    out, taken = [], 0
    for n, stage in enumerate(a):
        out.append(stage)
        due = (n + 1) * len(b) // len(a)
        out += b[taken:due]
        taken = due
    return out


def _mixer_kernel(dec_ref, sink_ref, q_ref, kt_ref, v_ref, s_ref, gain_ref,
                  qt_ref, k_ref, kp_ref, kn_ref, vt_ref, vtp_ref, vtn_ref,
                  z_ref, ot_ref, kv_ref, st_ref):
    j = pl.program_id(1)
    ret = _ret_stages(j, dec_ref, q_ref, kt_ref, v_ref, s_ref, gain_ref, z_ref, kv_ref, st_ref)
    att = _att_stages(j, pl.num_programs(1), sink_ref, qt_ref, k_ref, kp_ref, kn_ref,
                      vt_ref, vtp_ref, vtn_ref, ot_ref)
    for stage in _interleave(ret, att):
        stage()


def _mixer_call(dec, sink, qr, krt, vr, sr, gain, qat, k4, vat, batch, seq):
    t, ret_w = qr.shape
    att_w = qat.shape[0]
    tq = ATT_TILE
    n_tiles = seq // tq
    assert n_tiles == RET_HEADS, "one attention tile per retention head in each grid step"
    per_tile = tq // ATT_BLOCK
    n_halo = seq // ATT_BLOCK
    n_chunks = seq // RET_TILE

    def prev_blk(b, i):
        return b * n_halo + jnp.maximum(i * per_tile - 1, 0)

    def next_blk(b, i):
        return b * n_halo + jnp.minimum((i + 1) * per_tile, n_halo - 1)

    def dims_by_tokens(rows):
        return (pl.BlockSpec((rows, tq), lambda b, i: (0, b * n_tiles + i)),
                pl.BlockSpec((rows, ATT_BLOCK), lambda b, i: (0, prev_blk(b, i))),
                pl.BlockSpec((rows, ATT_BLOCK), lambda b, i: (0, next_blk(b, i))))

    kw = k4.shape[1]
    smem = pl.BlockSpec(memory_space=pltpu.SMEM)
    head = pl.BlockSpec((seq, RET_DK), lambda b, h: (b, h))
    return pl.pallas_call(
        _mixer_kernel,
        grid=(batch, n_tiles),
        in_specs=[smem, smem, head, pl.BlockSpec((RET_DK, seq), lambda b, h: (h, b)), head, head,
                  pl.BlockSpec((1, RET_DK), lambda b, h: (0, h)),
                  dims_by_tokens(att_w)[0],
                  pl.BlockSpec((tq, kw), lambda b, i: (b * n_tiles + i, 0)),
                  pl.BlockSpec((ATT_BLOCK, kw), lambda b, i: (prev_blk(b, i), 0)),
                  pl.BlockSpec((ATT_BLOCK, kw), lambda b, i: (next_blk(b, i), 0)),
                  *dims_by_tokens(vat.shape[0])],
        out_specs=[head, dims_by_tokens(att_w)[0]],
        out_shape=[jax.ShapeDtypeStruct((t, ret_w), BF16), jax.ShapeDtypeStruct((att_w, t), BF16)],
        scratch_shapes=[pltpu.VMEM((n_chunks, RET_DK, 2 * RET_DK), F32),
                        pltpu.VMEM((n_chunks, RET_DK, 2 * RET_DK), BF16)],
        compiler_params=_cparams(2),
        name="mixers",
    )(dec, sink, qr, krt, vr, sr, gain, qat, k4, k4, k4, vat, vat, vat)


def _tail_kernel(x_ref, zr_ref, oa_ref, sgr_ref, sga_ref, wro_ref, wao_ref, wout_ref,
                 gn_ref, wg_ref, wu_ref, wo_ref, gfin_ref, y_ref, h_ref, a_ref):
    y_ret = _dot(zr_ref[...], wro_ref[...])
    y_att = lax.dot_general(oa_ref[...], wao_ref[...], (((0,), (0,)), ((), ())), preferred_element_type=F32)
    merged = sgr_ref[...].astype(F32) * y_ret + sga_ref[...].astype(F32) * y_att
    x2 = x_ref[...] + _dot(merged.astype(BF16), wout_ref[...])
    group = h_ref.shape[0]
    for r in range(x2.shape[0] // group):
        rows = slice(r * group, (r + 1) * group)
        x3 = _swiglu_residual(x2[rows, :], gn_ref, wg_ref, wu_ref, wo_ref, h_ref, a_ref)
        y_ref[rows, :] = _rms(x3, gfin_ref[...])


def _tail_call(x1, zr, oa, sgr, sga, wro, wao, wout, gn, wg, wu, wo, gfin):
    t, d = x1.shape
    tm = TAIL_TILE
    ffp = wo.shape[0]
    tok = lambda width: pl.BlockSpec((tm, width), lambda i: (i, 0))
    return pl.pallas_call(
        _tail_kernel,
        grid=(t // tm,),
        in_specs=[tok(d), tok(zr.shape[1]), pl.BlockSpec((oa.shape[0], tm), lambda i: (0, i)), tok(d), tok(d),
                  _resident(wro.shape), _resident(wao.shape), _resident(wout.shape),
                  _resident(gn.shape), _resident(wg.shape), _resident(wu.shape), _resident(wo.shape),
                  _resident(gfin.shape)],
        out_specs=tok(d),
        out_shape=jax.ShapeDtypeStruct((t, d), F32),
        scratch_shapes=[pltpu.VMEM((TAIL_FFN_ROWS, d), BF16), pltpu.VMEM((TAIL_FFN_ROWS, ffp), BF16)],
        compiler_params=_cparams(1),
        name="tail",
    )(x1, zr, oa, sgr, sga, wro, wao, wout, gn, wg, wu, wo, gfin)


def _rope_tables(seq):
    pos = np.arange(seq, dtype=np.float32)[:, None]

    def cs(half):
        inv_freq = np.float32(ROPE_THETA) ** (-np.arange(half, dtype=np.float32) / np.float32(half))
        ang = (pos * inv_freq[None, :].astype(np.float32)).astype(np.float64)
        return np.cos(ang).astype(np.float32), np.sin(ang).astype(np.float32)

    c, s = cs(RET_DK // 2)
    cosr = np.concatenate([c, c], axis=1)
    sinr = np.concatenate([-s, s], axis=1)
    c, s = cs(ATT_HD // 2)
    cosa = np.concatenate([c, c, c, c], axis=1)
    sina = np.concatenate([-s, s, -s, s], axis=1)
    return tuple(jnp.asarray(tab) for tab in (cosr, sinr, cosa, sina))


def kernel(x, norm_ffn1, ffn1_w_in, ffn1_w_out, norm_mix, w_in, ret_decay_fwd, ret_decay_bwd,
           ret_gn_gain, w_ret_out, att_sink, w_att_out, w_out, norm_ffn2, ffn2_w_in, ffn2_w_out,
           norm_final):
    b, s, d = x.shape
    assert ffn1_w_in.shape[0] == 1, "single-layer block"
    assert all(s % tile == 0 for tile in (TAIL_TILE, FFN_TILE, PROJ_TILE, RET_TILE, ATT_TILE))
    assert TAIL_TILE % TAIL_FFN_ROWS == 0
    xt = x.reshape(b * s, d)
    cosr, sinr, cosa, sina = _rope_tables(s)

    x1, ffn2_w, (w_in_bf, w_ro_bf, w_ao_bf, w_out_bf) = _ffn_call(
        xt, norm_ffn1, ffn1_w_in[0], ffn1_w_out[0], ffn2_w_in[0], ffn2_w_out[0],
        (w_in[0], w_ret_out[0], w_att_out[0], w_out[0]))
    qr, krt, vr, sr, qat, k4, vat, sgr, sga = _proj_call(
        x1, norm_mix, w_in_bf, cosr, sinr, cosa, sina, s)
    dec = jnp.concatenate([ret_decay_fwd, ret_decay_bwd], axis=0)
    zr, oat = _mixer_call(dec, att_sink[0], qr, krt, vr, sr, ret_gn_gain, qat, k4, vat, b, s)
    y = _tail_call(x1, zr, oat, sgr, sga, w_ro_bf, w_ao_bf, w_out_bf, norm_ffn2, *ffn2_w,
                   norm_final.reshape(1, d))
    return y.reshape(b, s, d)
```

```python
import functools

import jax
import jax.numpy as jnp
import numpy as np
from jax import lax
from jax.experimental import pallas as pl
from jax.experimental.pallas import tpu as pltpu

F32 = jnp.float32
BF16 = jnp.bfloat16

EPS = 1e-6
LOG2_E = 1.4426950408889634
ROPE_THETA = 10000.0
RET_HEADS = 4
RET_DK = 128
ATT_Q_HEADS = 8
ATT_KV_HEADS = 2
ATT_HD = 64
ATT_BLOCK = 128

LANES = 128
BF16_SUBLANES = 16
MXU_WIDTH = 256
V7X_VMEM_BYTES = 64 * 1024 * 1024
VMEM_LIMIT = V7X_VMEM_BYTES - 6 * 1024 * 1024

FF_CHUNK = MXU_WIDTH
RET_TILE = 256
ATT_TILE = 1024
ATT_SCORES_AHEAD = 3
FFN_TILE = 1024
STAGE_IN_ROWS = 128
STAGE_OUT_ROWS = 512
PROJ_TILE = 1024
PROJ_RUN_ORDER = (6, 0, 7, 1, 3, 4, 5, 2)
TAIL_TILE = 1024
TAIL_FFN_ROWS = 512


def _cparams(n_grid):
    return pltpu.CompilerParams(dimension_semantics=("arbitrary",) * n_grid,
                                vmem_limit_bytes=VMEM_LIMIT)


def _resident(shape):
    zeros = (0,) * len(shape)
    return pl.BlockSpec(shape, lambda *_: zeros, pipeline_mode=pl.Buffered(1))


def _sigmoid(x):
    return 0.5 * jnp.tanh(0.5 * x) + 0.5


def _silu(x):
    return x * _sigmoid(x)


def _inv_rms(x):
    return lax.rsqrt(jnp.mean(x * x, axis=-1, keepdims=True) + EPS)


def _rms(x, g):
    return x * _inv_rms(x) * g


def _dot(a, b):
    return jnp.dot(a, b, preferred_element_type=F32)


def _swiglu_residual(x, gn_ref, wg_ref, wu_ref, wo_ref, h_ref, a_ref):
    h_ref[...] = (x * gn_ref[...]).astype(BF16)
    inv_rms = _inv_rms(x)
    fc = FF_CHUNK
    for c in range(wg_ref.shape[1] // fc):
        chunk = slice(c * fc, (c + 1) * fc)
        gu = _dot(h_ref[...], jnp.concatenate([wg_ref[:, chunk], wu_ref[:, chunk]], axis=1)) * inv_rms
        a_ref[:, chunk] = (_silu(gu[:, :fc]) * gu[:, fc:]).astype(BF16)
    return x + 0.5 * _dot(a_ref[...], wo_ref[...])


def _stream_rows(w_hbm, stage_ref, sem_ref, slab, consume):
    n_rows = w_hbm.shape[0]
    starts = list(range(0, n_rows, slab))

    def copy(n):
        size = min(slab, n_rows - starts[n])
        return pltpu.make_async_copy(w_hbm.at[pl.ds(starts[n], size), :],
                                     stage_ref.at[n % 2, pl.ds(0, size), :], sem_ref.at[n % 2])

    copy(0).start()
    for n, start in enumerate(starts):
        if n + 1 < len(starts):
            copy(n + 1).start()
        copy(n).wait()
        size = min(slab, n_rows - start)
        consume(slice(start, start + size), stage_ref.at[n % 2, pl.ds(0, size), :])


def _padded_ff(ff):
    return -(-ff // FF_CHUNK) * FF_CHUNK


def _split_gate_up(w, ff):
    ffp = _padded_ff(ff)
    up_from = 2 * ff - ffp
    assert up_from % LANES == 0
    real = lax.broadcasted_iota(jnp.int32, (w.shape[0], ffp), 1) < ff
    gate = jnp.where(real, w[:, :ffp], 0.0).astype(BF16)
    up = pltpu.roll(w[:, up_from:], ffp - (ff - up_from), 1)
    return gate, jnp.where(real, up, 0.0).astype(BF16)


def _stage_ffn_weights(win_hbm, wout_hbm, wg_ref, wu_ref, wo_ref, stage_in, stage_out, sem_in, sem_out):
    ff = wout_hbm.shape[0]
    assert ff % BF16_SUBLANES == 0

    def consume_in(rows, blk):
        wg_ref[rows, :], wu_ref[rows, :] = _split_gate_up(blk[...], ff)

    def consume_out(rows, blk):
        wo_ref[rows, :] = blk[...].astype(BF16)

    _stream_rows(win_hbm, stage_in, sem_in, stage_in.shape[1], consume_in)
    _stream_rows(wout_hbm, stage_out, sem_out, stage_out.shape[1], consume_out)
    wo_ref[ff:, :] = jnp.zeros((wo_ref.shape[0] - ff, wo_ref.shape[1]), BF16)


def _ffn_kernel(x_ref, gn_ref, win_hbm, wout_hbm, win2_ref, wout2_ref, *rest, ff, n_plain):
    plain_in, rest = rest[:n_plain], rest[n_plain:]
    o_ref, wg2_ref, wu2_ref, wo2_ref = rest[:4]
    plain_out, rest = rest[4:4 + n_plain], rest[4 + n_plain:]
    h_ref, a_ref, wg_ref, wu_ref, wo_ref, stage_in, stage_out, sem_in, sem_out = rest

    @pl.when(pl.program_id(0) == 0)
    def _():
        _stage_ffn_weights(win_hbm, wout_hbm, wg_ref, wu_ref, wo_ref, stage_in, stage_out, sem_in, sem_out)

    o_ref[...] = _swiglu_residual(x_ref[...], gn_ref, wg_ref, wu_ref, wo_ref, h_ref, a_ref)

    wg2_ref[...], wu2_ref[...] = _split_gate_up(win2_ref[...], ff)
    slab = wout2_ref.shape[0]
    row = (pl.program_id(0) // 2) * slab + lax.broadcasted_iota(jnp.int32, wout2_ref.shape, 0)
    wo2_ref[...] = jnp.where(row < ff, wout2_ref[...], 0.0).astype(BF16)
    for src, dst in zip(plain_in, plain_out):
        dst[...] = src[...].astype(BF16)


def _ffn_call(x, gn, w_in, w_out, w_in2, w_out2, plain):
    t, d = x.shape
    tm = FFN_TILE
    n = t // tm
    ff = w_out.shape[0]
    ffp = _padded_ff(ff)
    assert n % 2 == 0 and w_in2.shape == w_in.shape and w_out2.shape == w_out.shape
    tok = pl.BlockSpec((tm, d), lambda i: (i, 0))
    hbm = pl.BlockSpec(memory_space=pl.ANY)

    def slab(rows, width):
        assert rows % n == 0 and (rows // n) % BF16_SUBLANES == 0
        return pl.BlockSpec((rows // n, width), lambda i: (i, 0))

    down_rows = ffp // (n // 2)
    assert ffp % (n // 2) == 0 and down_rows % BF16_SUBLANES == 0
    down = pl.BlockSpec((down_rows, d), lambda i: (i // 2, 0))
    plain_specs = [slab(*w.shape) for w in plain]
    outs = pl.pallas_call(
        functools.partial(_ffn_kernel, ff=ff, n_plain=len(plain)),
        grid=(n,),
        in_specs=[tok, _resident(gn.shape), hbm, hbm, slab(d, 2 * ff), down] + plain_specs,
        out_specs=[tok, slab(d, ffp), slab(d, ffp), down] + plain_specs,
        out_shape=[jax.ShapeDtypeStruct((t, d), F32), jax.ShapeDtypeStruct((d, ffp), BF16),
                   jax.ShapeDtypeStruct((d, ffp), BF16), jax.ShapeDtypeStruct((ffp, d), BF16)]
                  + [jax.ShapeDtypeStruct(w.shape, BF16) for w in plain],
        scratch_shapes=[pltpu.VMEM((tm, d), BF16), pltpu.VMEM((tm, ffp), BF16),
                        pltpu.VMEM((d, ffp), BF16), pltpu.VMEM((d, ffp), BF16), pltpu.VMEM((ffp, d), BF16),
                        pltpu.VMEM((2, STAGE_IN_ROWS, 2 * ff), F32), pltpu.VMEM((2, STAGE_OUT_ROWS, d), F32),
                        pltpu.SemaphoreType.DMA((2,)), pltpu.SemaphoreType.DMA((2,))],
        compiler_params=_cparams(1),
        name="ffn1",
    )(x, gn, w_in, w_out, w_in2, w_out2, *plain)
    return outs[0], tuple(outs[1:4]), tuple(outs[4:])


def _rope128(y, cos, sin_signed):
    return y * cos + pltpu.roll(y, RET_DK // 2, 1) * sin_signed


def _rope64(y, cos, sin_signed):
    first_half = (lax.broadcasted_iota(jnp.int32, y.shape, 1) // (ATT_HD // 2)) % 2 == 0
    partner = jnp.where(first_half, pltpu.roll(y, LANES - ATT_HD // 2, 1), pltpu.roll(y, ATT_HD // 2, 1))
    return y * cos + partner * sin_signed


def _proj_kernel(x_ref, gn_ref, w_ref, cosr_ref, sinr_ref, cosa_ref, sina_ref,
                 qr_ref, kr_ref, vr_ref, sr_ref, qa_ref, k4_ref, va_ref, sgr_ref, sga_ref,
                 h_ref, *, ret_w, att_qw):
    h_ref[...] = (x_ref[...] * gn_ref[...]).astype(BF16)
    inv_rms = _inv_rms(x_ref[...])

    def proj(a, b):
        return _dot(h_ref[...], w_ref[:, a:b]) * inv_rms

    cosr, sinr = cosr_ref[...], sinr_ref[...]
    cosa, sina = cosa_ref[...], sina_ref[...]
    d = x_ref.shape[1]
    lane_blocks = [slice(j * LANES, (j + 1) * LANES) for j in range(ret_w // LANES)]
    assert att_qw == ret_w

    def ret_q(c0):
        y = proj(c0, c0 + ret_w)
        for sl in lane_blocks:
            qr_ref[:, sl] = _rope128(y[:, sl], cosr, sinr).astype(BF16)

    def ret_k(c0):
        y = proj(c0, c0 + ret_w)
        for sl in lane_blocks:
            kr_ref[sl, :] = (_rope128(y[:, sl], cosr, sinr) * (RET_DK ** -0.5)).T.astype(BF16)

    def ret_v(c0):
        vr_ref[...] = proj(c0, c0 + ret_w).astype(BF16)

    def ret_g(c0):
        sr_ref[...] = _silu(proj(c0, c0 + ret_w)).astype(BF16)

    def att_q(c0):
        y = proj(c0, c0 + att_qw)
        for sl in lane_blocks:
            qa_ref[sl, :] = (_rope64(y[:, sl], cosa, sina) * (ATT_HD ** -0.5 * LOG2_E)).T.astype(BF16)

    def att_kv(c0):
        y = proj(c0, c0 + 2 * LANES)
        kk = _rope64(y[:, :LANES], cosa, sina)
        low = lax.broadcasted_iota(jnp.int32, kk.shape, 1) < ATT_HD
        kswap = pltpu.roll(kk, ATT_HD, 1)
        k4_ref[:, 0 * LANES:1 * LANES] = jnp.where(low, kk, 0.0).astype(BF16)
        k4_ref[:, 1 * LANES:2 * LANES] = jnp.where(low, 0.0, kswap).astype(BF16)
        k4_ref[:, 2 * LANES:3 * LANES] = jnp.where(low, kswap, 0.0).astype(BF16)
        k4_ref[:, 3 * LANES:4 * LANES] = jnp.where(low, 0.0, kk).astype(BF16)
        va_ref[...] = y[:, LANES:].T.astype(BF16)

    def gate_r(c0):
        sgr_ref[...] = _sigmoid(proj(c0, c0 + d)).astype(BF16)

    def gate_a(c0):
        sga_ref[...] = _sigmoid(proj(c0, c0 + d)).astype(BF16)

    groups = [(ret_q, ret_w), (ret_k, ret_w), (ret_v, ret_w), (ret_g, ret_w), (att_q, att_qw),
              (att_kv, 2 * LANES), (gate_r, d), (gate_a, d)]
    starts = np.cumsum([0] + [width for _, width in groups])
    for n in PROJ_RUN_ORDER:
        groups[n][0](int(starts[n]))


def _proj_call(x1, gn, w, cosr, sinr, cosa, sina, seq):
    t, d = x1.shape
    tm = PROJ_TILE
    ret_w = RET_HEADS * RET_DK
    att_qw = ATT_Q_HEADS * ATT_HD
    n_pos = seq // tm
    tok = lambda width: pl.BlockSpec((tm, width), lambda i: (i, 0))
    tab = pl.BlockSpec((tm, LANES), lambda i: (i % n_pos, 0))
    widths = (ret_w, ret_w, ret_w, ret_w, att_qw, 4 * LANES, LANES, d, d)
    out_specs = [tok(wd) for wd in widths]
    out_shape = [jax.ShapeDtypeStruct((t, wd), BF16) for wd in widths]
    for k_out in (1, 4, 6):
        out_specs[k_out] = pl.BlockSpec((widths[k_out], tm), lambda i: (0, i))
        out_shape[k_out] = jax.ShapeDtypeStruct((widths[k_out], t), BF16)
    return pl.pallas_call(
        functools.partial(_proj_kernel, ret_w=ret_w, att_qw=att_qw),
        grid=(t // tm,),
        in_specs=[tok(d), _resident(gn.shape), _resident(w.shape), tab, tab, tab, tab],
        out_specs=out_specs,
        out_shape=out_shape,
        scratch_shapes=[pltpu.VMEM((tm, d), BF16)],
        compiler_params=_cparams(1),
        name="proj",
    )(x1, gn, w, cosr, sinr, cosa, sina)


def _ret_stages(h, dec_ref, q_ref, kt_ref, v_ref, s_ref, gain_ref, z_ref, kv_ref, st_ref):
    c = RET_TILE
    dk = kt_ref.shape[0]
    n_chunks = q_ref.shape[0] // c

    def log_gamma(direction, shape):
        return -jnp.exp(jnp.full(shape, dec_ref[direction, h], F32))

    def iota(shape, axis):
        return lax.broadcasted_iota(jnp.int32, shape, axis).astype(F32)

    rel = iota((c, c), 0) - iota((c, c), 1)
    decay = jnp.where(rel >= 0, jnp.exp(jnp.maximum(rel, 0.0) * log_gamma(0, (c, c))),
                      jnp.exp(jnp.maximum(-rel, 0.0) * log_gamma(1, (c, c))))
    pos = iota((c, dk), 0)
    kdec_f = jnp.exp((c - 1 - pos) * log_gamma(0, (c, dk)))
    kdec_b = jnp.exp(pos * log_gamma(1, (c, dk)))
    qdec_f = jnp.exp((pos + 1.0) * log_gamma(0, (c, dk)))
    qdec_b = jnp.exp((c - pos) * log_gamma(1, (c, dk)))
    g_f = jnp.exp(c * log_gamma(0, (dk, dk)))
    g_b = jnp.exp(c * log_gamma(1, (dk, dk)))

    def span(i):
        return slice(i * c, (i + 1) * c)

    def kv_stage(i):
        v = v_ref[span(i), :].astype(F32)
        rhs = jnp.concatenate([(v * kdec_f).astype(BF16), (v * kdec_b).astype(BF16)], axis=1)
        kv_ref[i] = _dot(kt_ref[:, span(i)], rhs)

    def state_stage():
        state = jnp.zeros((dk, dk), F32)
        for i in range(n_chunks):
            st_ref[i, :, :dk] = state.astype(BF16)
            state = g_f * state + kv_ref[i, :, :dk]
        state = jnp.zeros((dk, dk), F32)
        for i in reversed(range(n_chunks)):
            st_ref[i, :, dk:] = state.astype(BF16)
            state = g_b * state + kv_ref[i, :, dk:]

    def out_stage(i):
        q = q_ref[span(i), :]
        scores = _dot(q, kt_ref[:, span(i)])
        inner = _dot((scores * decay).astype(BF16), v_ref[span(i), :])
        cross = _dot(q, st_ref[i])
        y = inner + qdec_f * cross[:, :dk] + qdec_b * cross[:, dk:]
        mu = jnp.mean(y, axis=-1, keepdims=True)
        yc = y - mu
        var = jnp.mean(yc * yc, axis=-1, keepdims=True)
        yn = yc * lax.rsqrt(var + EPS) * gain_ref[...]
        z_ref[span(i), :] = (yn * s_ref[span(i), :].astype(F32)).astype(BF16)

    return ([functools.partial(kv_stage, i) for i in range(n_chunks)] + [state_stage]
            + [functools.partial(out_stage, i) for i in range(n_chunks)])


def _att_stages(tile, n_tiles, sink_ref, qt_ref, k_ref, kp_ref, kn_ref, vt_ref, vtp_ref, vtn_ref, ot_ref):
    blk = ATT_BLOCK
    n_blocks = qt_ref.shape[1] // blk
    grp = ATT_Q_HEADS // ATT_KV_HEADS
    hd = ATT_HD
    neg_inf = jnp.float32(-jnp.inf)
    kpos = lax.broadcasted_iota(jnp.int32, (blk, 2 * blk), 0)
    qpos = lax.broadcasted_iota(jnp.int32, (blk, 2 * blk), 1) % blk
    prev_bias = jnp.where(kpos >= qpos, 0.0, neg_inf)
    next_bias = jnp.where(kpos <= qpos, 0.0, neg_inf)
    first_prev_bias = prev_bias + jnp.where(tile > 0, 0.0, neg_inf)
    last_next_bias = next_bias + jnp.where(tile < n_tiles - 1, 0.0, neg_inf)
    left = lax.broadcasted_iota(jnp.int32, (1, 2 * blk), 1) < blk

    def cols(j):
        return slice(j * blk, (j + 1) * blk)

    def scores(j, g, p):
        qt = jnp.concatenate([qt_ref[cols(2 * g), cols(j)], qt_ref[cols(2 * g + 1), cols(j)]], axis=1)
        kv = cols(2 * g + p)
        kwin = jnp.concatenate([kp_ref[:, kv] if j == 0 else k_ref[cols(j - 1), kv], k_ref[cols(j), kv],
                                kn_ref[:, kv] if j == n_blocks - 1 else k_ref[cols(j + 1), kv]], axis=0)
        return _dot(kwin, qt)

    def finish(j, g, p, s):
        s_prev = s[cols(0), :] + (first_prev_bias if j == 0 else prev_bias)
        s_cur = s[cols(1), :]
        s_next = s[cols(2), :] + (last_next_bias if j == n_blocks - 1 else next_bias)
        snk = jnp.where(left, sink_ref[grp * g + p], sink_ref[grp * g + 2 + p]) * LOG2_E
        m = jnp.maximum(jnp.max(jnp.maximum(jnp.maximum(s_prev, s_cur), s_next), axis=0, keepdims=True), snk)
        et = jnp.concatenate([jnp.exp2(s_prev - m).astype(BF16), jnp.exp2(s_cur - m).astype(BF16),
                              jnp.exp2(s_next - m).astype(BF16)], axis=0)
        dims = slice(g * hd, (g + 1) * hd)
        vt = jnp.concatenate([vtp_ref[dims, :] if j == 0 else vt_ref[dims, cols(j - 1)],
                              vt_ref[dims, cols(j)],
                              vtn_ref[dims, :] if j == n_blocks - 1 else vt_ref[dims, cols(j + 1)]], axis=1)
        acc = _dot(jnp.concatenate([vt, jnp.ones((BF16_SUBLANES, 3 * blk), BF16)], axis=0), et)
        denom = acc[hd:hd + 1, :] + jnp.exp2(snk - m)
        ot = acc[:hd, :] * (1.0 / denom)
        head_a = grp * g + p
        head_b = grp * g + 2 + p
        ot_ref[head_a * hd:(head_a + 1) * hd, cols(j)] = ot[:, :blk].astype(BF16)
        ot_ref[head_b * hd:(head_b + 1) * hd, cols(j)] = ot[:, blk:].astype(BF16)

    units = [(j, g, p) for j in range(n_blocks) for g in range(ATT_KV_HEADS) for p in range(2)]
    pending = []

    def stage(n):
        if n == 0:
            pending.extend(scores(*u) for u in units[:ATT_SCORES_AHEAD])
        if n + ATT_SCORES_AHEAD < len(units):
            pending.append(scores(*units[n + ATT_SCORES_AHEAD]))
        finish(*units[n], pending.pop(0))

    return [functools.partial(stage, n) for n in range(len(units))]


def _interleave(a, b):
    if len(a) < len(b):
        a, b = b, a
    out, taken = [], 0
    for n, stage in enumerate(a):
        out.append(stage)
        due = (n + 1) * len(b) // len(a)
        out += b[taken:due]
        taken = due
    return out


def _mixer_kernel(dec_ref, sink_ref, q_ref, kt_ref, v_ref, s_ref, gain_ref,
                  qt_ref, k_ref, kp_ref, kn_ref, vt_ref, vtp_ref, vtn_ref,
                  z_ref, ot_ref, kv_ref, st_ref):
    j = pl.program_id(1)
    ret = _ret_stages(j, dec_ref, q_ref, kt_ref, v_ref, s_ref, gain_ref, z_ref, kv_ref, st_ref)
    att = _att_stages(j, pl.num_programs(1), sink_ref, qt_ref, k_ref, kp_ref, kn_ref,
                      vt_ref, vtp_ref, vtn_ref, ot_ref)
    for stage in [att[0]] + _interleave(ret, att[1:]):
        stage()


def _mixer_call(dec, sink, qr, krt, vr, sr, gain, qat, k4, vat, batch, seq):
    t, ret_w = qr.shape
    att_w = qat.shape[0]
    tq = ATT_TILE
    n_tiles = seq // tq
    assert n_tiles == RET_HEADS, "one attention tile per retention head in each grid step"
    per_tile = tq // ATT_BLOCK
    n_halo = seq // ATT_BLOCK
    n_chunks = seq // RET_TILE

    def prev_blk(b, i):
        return b * n_halo + jnp.maximum(i * per_tile - 1, 0)

    def next_blk(b, i):
        return b * n_halo + jnp.minimum((i + 1) * per_tile, n_halo - 1)

    def dims_by_tokens(rows):
        return (pl.BlockSpec((rows, tq), lambda b, i: (0, b * n_tiles + i)),
                pl.BlockSpec((rows, ATT_BLOCK), lambda b, i: (0, prev_blk(b, i))),
                pl.BlockSpec((rows, ATT_BLOCK), lambda b, i: (0, next_blk(b, i))))

    kw = k4.shape[1]
    smem = pl.BlockSpec(memory_space=pltpu.SMEM)
    head = pl.BlockSpec((seq, RET_DK), lambda b, h: (b, h))
    return pl.pallas_call(
        _mixer_kernel,
        grid=(batch, n_tiles),
        in_specs=[smem, smem, head, pl.BlockSpec((RET_DK, seq), lambda b, h: (h, b)), head, head,
                  pl.BlockSpec((1, RET_DK), lambda b, h: (0, h)),
                  dims_by_tokens(att_w)[0],
                  pl.BlockSpec((tq, kw), lambda b, i: (b * n_tiles + i, 0)),
                  pl.BlockSpec((ATT_BLOCK, kw), lambda b, i: (prev_blk(b, i), 0)),
                  pl.BlockSpec((ATT_BLOCK, kw), lambda b, i: (next_blk(b, i), 0)),
                  *dims_by_tokens(vat.shape[0])],
        out_specs=[head, dims_by_tokens(att_w)[0]],
        out_shape=[jax.ShapeDtypeStruct((t, ret_w), BF16), jax.ShapeDtypeStruct((att_w, t), BF16)],
        scratch_shapes=[pltpu.VMEM((n_chunks, RET_DK, 2 * RET_DK), F32),
                        pltpu.VMEM((n_chunks, RET_DK, 2 * RET_DK), BF16)],
        compiler_params=_cparams(2),
        name="mixers",
    )(dec, sink, qr, krt, vr, sr, gain, qat, k4, k4, k4, vat, vat, vat)


def _tail_kernel(x_ref, zr_ref, oa_ref, sgr_ref, sga_ref, wro_ref, wao_ref, wout_ref,
                 gn_ref, wg_ref, wu_ref, wo_ref, gfin_ref, y_ref, h_ref, a_ref):
    group = h_ref.shape[0]
    groups = [slice(r * group, (r + 1) * group) for r in range(x_ref.shape[0] // group)]
    y_ret, y_att = [], []
    for rows in groups:
        y_ret.append(_dot(zr_ref[rows, :], wro_ref[...]))
        y_att.append(lax.dot_general(oa_ref[:, rows], wao_ref[...], (((0,), (0,)), ((), ())),
                                     preferred_element_type=F32))
    x2 = []
    for rows, yr, ya in zip(groups, y_ret, y_att):
        merged = sgr_ref[rows, :].astype(F32) * yr + sga_ref[rows, :].astype(F32) * ya
        x2.append(x_ref[rows, :] + _dot(merged.astype(BF16), wout_ref[...]))
    for rows, x2_rows in zip(groups, x2):
        x3 = _swiglu_residual(x2_rows, gn_ref, wg_ref, wu_ref, wo_ref, h_ref, a_ref)
        y_ref[rows, :] = _rms(x3, gfin_ref[...])


def _tail_call(x1, zr, oa, sgr, sga, wro, wao, wout, gn, wg, wu, wo, gfin):
    t, d = x1.shape
    tm = TAIL_TILE
    ffp = wo.shape[0]
    tok = lambda width: pl.BlockSpec((tm, width), lambda i: (i, 0))
    return pl.pallas_call(
        _tail_kernel,
        grid=(t // tm,),
        in_specs=[tok(d), tok(zr.shape[1]), pl.BlockSpec((oa.shape[0], tm), lambda i: (0, i)), tok(d), tok(d),
                  _resident(wro.shape), _resident(wao.shape), _resident(wout.shape),
                  _resident(gn.shape), _resident(wg.shape), _resident(wu.shape), _resident(wo.shape),
                  _resident(gfin.shape)],
        out_specs=tok(d),
        out_shape=jax.ShapeDtypeStruct((t, d), F32),
        scratch_shapes=[pltpu.VMEM((TAIL_FFN_ROWS, d), BF16), pltpu.VMEM((TAIL_FFN_ROWS, ffp), BF16)],
        compiler_params=_cparams(1),
        name="tail",
    )(x1, zr, oa, sgr, sga, wro, wao, wout, gn, wg, wu, wo, gfin)


def _rope_tables(seq):
    pos = np.arange(seq, dtype=np.float32)[:, None]

    def cs(half):
        inv_freq = np.float32(ROPE_THETA) ** (-np.arange(half, dtype=np.float32) / np.float32(half))
        ang = (pos * inv_freq[None, :].astype(np.float32)).astype(np.float64)
        return np.cos(ang).astype(np.float32), np.sin(ang).astype(np.float32)

    c, s = cs(RET_DK // 2)
    cosr = np.concatenate([c, c], axis=1)
    sinr = np.concatenate([-s, s], axis=1)
    c, s = cs(ATT_HD // 2)
    cosa = np.concatenate([c, c, c, c], axis=1)
    sina = np.concatenate([-s, s, -s, s], axis=1)
    return tuple(jnp.asarray(tab) for tab in (cosr, sinr, cosa, sina))


def kernel(x, norm_ffn1, ffn1_w_in, ffn1_w_out, norm_mix, w_in, ret_decay_fwd, ret_decay_bwd,
           ret_gn_gain, w_ret_out, att_sink, w_att_out, w_out, norm_ffn2, ffn2_w_in, ffn2_w_out,
           norm_final):
    b, s, d = x.shape
    assert ffn1_w_in.shape[0] == 1, "single-layer block"
    assert all(s % tile == 0 for tile in (TAIL_TILE, FFN_TILE, PROJ_TILE, RET_TILE, ATT_TILE))
    assert TAIL_TILE % TAIL_FFN_ROWS == 0
    xt = x.reshape(b * s, d)
    cosr, sinr, cosa, sina = _rope_tables(s)

    x1, ffn2_w, (w_in_bf, w_ro_bf, w_ao_bf, w_out_bf) = _ffn_call(
        xt, norm_ffn1, ffn1_w_in[0], ffn1_w_out[0], ffn2_w_in[0], ffn2_w_out[0],
        (w_in[0], w_ret_out[0], w_att_out[0], w_out[0]))
    qr, krt, vr, sr, qat, k4, vat, sgr, sga = _proj_call(
        x1, norm_mix, w_in_bf, cosr, sinr, cosa, sina, s)
    dec = jnp.concatenate([ret_decay_fwd, ret_decay_bwd], axis=0)
    zr, oat = _mixer_call(dec, att_sink[0], qr, krt, vr, sr, ret_gn_gain, qat, k4, vat, b, s)
    y = _tail_call(x1, zr, oat, sgr, sga, w_ro_bf, w_ao_bf, w_out_bf, norm_ffn2, *ffn2_w,
                   norm_final.reshape(1, d))
    return y.reshape(b, s, d)
```

```python
import functools

import jax
import jax.numpy as jnp
import numpy as np
from jax import lax
from jax.experimental import pallas as pl
from jax.experimental.pallas import tpu as pltpu

F32 = jnp.float32
BF16 = jnp.bfloat16

EPS = 1e-6
LOG2_E = 1.4426950408889634
ROPE_THETA = 10000.0
RET_HEADS = 4
RET_DK = 128
ATT_Q_HEADS = 8
ATT_KV_HEADS = 2
ATT_HD = 64
ATT_BLOCK = 128

LANES = 128
BF16_SUBLANES = 16
MXU_WIDTH = 256
V7X_VMEM_BYTES = 64 * 1024 * 1024
VMEM_LIMIT = V7X_VMEM_BYTES - 6 * 1024 * 1024

FF_CHUNK = MXU_WIDTH
RET_TILE = 256
ATT_TILE = 1024
ATT_SCORES_AHEAD = 3
FFN_TILE = 1024
STAGE_IN_ROWS = 128
STAGE_OUT_ROWS = 512
PROJ_TILE = 1024
PROJ_RUN_ORDER = (6, 0, 7, 1, 3, 4, 5, 2)
TAIL_TILE = 1024
TAIL_FFN_ROWS = 512


def _cparams(n_grid):
    return pltpu.CompilerParams(dimension_semantics=("arbitrary",) * n_grid,
                                vmem_limit_bytes=VMEM_LIMIT)


def _resident(shape):
    zeros = (0,) * len(shape)
    return pl.BlockSpec(shape, lambda *_: zeros, pipeline_mode=pl.Buffered(1))


def _sigmoid(x):
    return 0.5 * jnp.tanh(0.5 * x) + 0.5


def _silu(x):
    return x * _sigmoid(x)


def _inv_rms(x):
    return lax.rsqrt(jnp.mean(x * x, axis=-1, keepdims=True) + EPS)


def _rms(x, g):
    return x * _inv_rms(x) * g


def _dot(a, b):
    return jnp.dot(a, b, preferred_element_type=F32)


def _swiglu_residual(x, gn_ref, wg_ref, wu_ref, wo_ref, h_ref, a_ref):
    h_ref[...] = (x * gn_ref[...]).astype(BF16)
    inv_rms = _inv_rms(x)
    fc = FF_CHUNK
    for c in range(wg_ref.shape[1] // fc):
        chunk = slice(c * fc, (c + 1) * fc)
        gu = _dot(h_ref[...], jnp.concatenate([wg_ref[:, chunk], wu_ref[:, chunk]], axis=1)) * inv_rms
        a_ref[:, chunk] = (_silu(gu[:, :fc]) * gu[:, fc:]).astype(BF16)
    return x + 0.5 * _dot(a_ref[...], wo_ref[...])


def _stream_rows(w_hbm, stage_ref, sem_ref, slab, consume):
    n_rows = w_hbm.shape[0]
    starts = list(range(0, n_rows, slab))

    def copy(n):
        size = min(slab, n_rows - starts[n])
        return pltpu.make_async_copy(w_hbm.at[pl.ds(starts[n], size), :],
                                     stage_ref.at[n % 2, pl.ds(0, size), :], sem_ref.at[n % 2])

    copy(0).start()
    for n, start in enumerate(starts):
        if n + 1 < len(starts):
            copy(n + 1).start()
        copy(n).wait()
        size = min(slab, n_rows - start)
        consume(slice(start, start + size), stage_ref.at[n % 2, pl.ds(0, size), :])


def _padded_ff(ff):
    return -(-ff // FF_CHUNK) * FF_CHUNK


def _split_gate_up(w, ff):
    ffp = _padded_ff(ff)
    up_from = 2 * ff - ffp
    assert up_from % LANES == 0
    real = lax.broadcasted_iota(jnp.int32, (w.shape[0], ffp), 1) < ff
    gate = jnp.where(real, w[:, :ffp], 0.0).astype(BF16)
    up = pltpu.roll(w[:, up_from:], ffp - (ff - up_from), 1)
    return gate, jnp.where(real, up, 0.0).astype(BF16)


def _stage_ffn_weights(win_hbm, wout_hbm, wg_ref, wu_ref, wo_ref, stage_in, stage_out, sem_in, sem_out):
    ff = wout_hbm.shape[0]
    assert ff % BF16_SUBLANES == 0

    def consume_in(rows, blk):
        wg_ref[rows, :], wu_ref[rows, :] = _split_gate_up(blk[...], ff)

    def consume_out(rows, blk):
        wo_ref[rows, :] = blk[...].astype(BF16)

    _stream_rows(win_hbm, stage_in, sem_in, stage_in.shape[1], consume_in)
    _stream_rows(wout_hbm, stage_out, sem_out, stage_out.shape[1], consume_out)
    wo_ref[ff:, :] = jnp.zeros((wo_ref.shape[0] - ff, wo_ref.shape[1]), BF16)


def _ffn_kernel(x_ref, gn_ref, win_hbm, wout_hbm, win2_ref, wout2_ref, *rest, ff, n_plain):
    plain_in, rest = rest[:n_plain], rest[n_plain:]
    o_ref, wg2_ref, wu2_ref, wo2_ref = rest[:4]
    plain_out, rest = rest[4:4 + n_plain], rest[4 + n_plain:]
    h_ref, a_ref, wg_ref, wu_ref, wo_ref, stage_in, stage_out, sem_in, sem_out = rest

    @pl.when(pl.program_id(0) == 0)
    def _():
        _stage_ffn_weights(win_hbm, wout_hbm, wg_ref, wu_ref, wo_ref, stage_in, stage_out, sem_in, sem_out)

    o_ref[...] = _swiglu_residual(x_ref[...], gn_ref, wg_ref, wu_ref, wo_ref, h_ref, a_ref)

    wg2_ref[...], wu2_ref[...] = _split_gate_up(win2_ref[...], ff)
    slab = wout2_ref.shape[0]
    row = (pl.program_id(0) // 2) * slab + lax.broadcasted_iota(jnp.int32, wout2_ref.shape, 0)
    wo2_ref[...] = jnp.where(row < ff, wout2_ref[...], 0.0).astype(BF16)
    for src, dst in zip(plain_in, plain_out):
        dst[...] = src[...].astype(BF16)


def _ffn_call(x, gn, w_in, w_out, w_in2, w_out2, plain):
    t, d = x.shape
    tm = FFN_TILE
    n = t // tm
    ff = w_out.shape[0]
    ffp = _padded_ff(ff)
    assert n % 2 == 0 and w_in2.shape == w_in.shape and w_out2.shape == w_out.shape
    tok = pl.BlockSpec((tm, d), lambda i: (i, 0))
    hbm = pl.BlockSpec(memory_space=pl.ANY)

    def slab(rows, width):
        assert rows % n == 0 and (rows // n) % BF16_SUBLANES == 0
        return pl.BlockSpec((rows // n, width), lambda i: (i, 0))

    down_rows = ffp // (n // 2)
    assert ffp % (n // 2) == 0 and down_rows % BF16_SUBLANES == 0
    down = pl.BlockSpec((down_rows, d), lambda i: (i // 2, 0))
    plain_specs = [slab(*w.shape) for w in plain]
    outs = pl.pallas_call(
        functools.partial(_ffn_kernel, ff=ff, n_plain=len(plain)),
        grid=(n,),
        in_specs=[tok, _resident(gn.shape), hbm, hbm, slab(d, 2 * ff), down] + plain_specs,
        out_specs=[tok, slab(d, ffp), slab(d, ffp), down] + plain_specs,
        out_shape=[jax.ShapeDtypeStruct((t, d), F32), jax.ShapeDtypeStruct((d, ffp), BF16),
                   jax.ShapeDtypeStruct((d, ffp), BF16), jax.ShapeDtypeStruct((ffp, d), BF16)]
                  + [jax.ShapeDtypeStruct(w.shape, BF16) for w in plain],
        scratch_shapes=[pltpu.VMEM((tm, d), BF16), pltpu.VMEM((tm, ffp), BF16),
                        pltpu.VMEM((d, ffp), BF16), pltpu.VMEM((d, ffp), BF16), pltpu.VMEM((ffp, d), BF16),
                        pltpu.VMEM((2, STAGE_IN_ROWS, 2 * ff), F32), pltpu.VMEM((2, STAGE_OUT_ROWS, d), F32),
                        pltpu.SemaphoreType.DMA((2,)), pltpu.SemaphoreType.DMA((2,))],
        compiler_params=_cparams(1),
        name="ffn1",
    )(x, gn, w_in, w_out, w_in2, w_out2, *plain)
    return outs[0], tuple(outs[1:4]), tuple(outs[4:])


def _rope128(y, cos, sin_signed):
    return y * cos + pltpu.roll(y, RET_DK // 2, 1) * sin_signed


def _rope64(y, cos, sin_signed):
    first_half = (lax.broadcasted_iota(jnp.int32, y.shape, 1) // (ATT_HD // 2)) % 2 == 0
    partner = jnp.where(first_half, pltpu.roll(y, LANES - ATT_HD // 2, 1), pltpu.roll(y, ATT_HD // 2, 1))
    return y * cos + partner * sin_signed


def _proj_kernel(x_ref, gn_ref, w_ref, cosr_ref, sinr_ref, cosa_ref, sina_ref,
                 qr_ref, kr_ref, vr_ref, sr_ref, qa_ref, k4_ref, va_ref, sgr_ref, sga_ref,
                 h_ref, *, ret_w, att_qw):
    h_ref[...] = (x_ref[...] * gn_ref[...]).astype(BF16)
    inv_rms = _inv_rms(x_ref[...])

    def proj(a, b):
        return _dot(h_ref[...], w_ref[:, a:b]) * inv_rms

    cosr, sinr = cosr_ref[...], sinr_ref[...]
    cosa, sina = cosa_ref[...], sina_ref[...]
    d = x_ref.shape[1]
    lane_blocks = [slice(j * LANES, (j + 1) * LANES) for j in range(ret_w // LANES)]
    assert att_qw == ret_w

    def ret_q(c0):
        y = proj(c0, c0 + ret_w)
        for sl in lane_blocks:
            qr_ref[:, sl] = _rope128(y[:, sl], cosr, sinr).astype(BF16)

    def ret_k(c0):
        y = proj(c0, c0 + ret_w)
        for sl in lane_blocks:
            kr_ref[sl, :] = (_rope128(y[:, sl], cosr, sinr) * (RET_DK ** -0.5)).T.astype(BF16)

    def ret_v(c0):
        vr_ref[...] = proj(c0, c0 + ret_w).astype(BF16)

    def ret_g(c0):
        sr_ref[...] = _silu(proj(c0, c0 + ret_w)).astype(BF16)

    def att_q(c0):
        y = proj(c0, c0 + att_qw)
        for sl in lane_blocks:
            qa_ref[sl, :] = (_rope64(y[:, sl], cosa, sina) * (ATT_HD ** -0.5 * LOG2_E)).T.astype(BF16)

    def att_kv(c0):
        y = proj(c0, c0 + 2 * LANES)
        kk = _rope64(y[:, :LANES], cosa, sina)
        low = lax.broadcasted_iota(jnp.int32, kk.shape, 1) < ATT_HD
        kswap = pltpu.roll(kk, ATT_HD, 1)
        k4_ref[:, 0 * LANES:1 * LANES] = jnp.where(low, kk, 0.0).astype(BF16)
        k4_ref[:, 1 * LANES:2 * LANES] = jnp.where(low, 0.0, kswap).astype(BF16)
        k4_ref[:, 2 * LANES:3 * LANES] = jnp.where(low, kswap, 0.0).astype(BF16)
        k4_ref[:, 3 * LANES:4 * LANES] = jnp.where(low, 0.0, kk).astype(BF16)
        va_ref[...] = y[:, LANES:].T.astype(BF16)

    def gate_r(c0):
        sgr_ref[...] = _sigmoid(proj(c0, c0 + d)).astype(BF16)

    def gate_a(c0):
        sga_ref[...] = _sigmoid(proj(c0, c0 + d)).astype(BF16)

    groups = [(ret_q, ret_w), (ret_k, ret_w), (ret_v, ret_w), (ret_g, ret_w), (att_q, att_qw),
              (att_kv, 2 * LANES), (gate_r, d), (gate_a, d)]
    starts = np.cumsum([0] + [width for _, width in groups])
    for n in PROJ_RUN_ORDER:
        groups[n][0](int(starts[n]))


def _proj_call(x1, gn, w, cosr, sinr, cosa, sina, seq):
    t, d = x1.shape
    tm = PROJ_TILE
    ret_w = RET_HEADS * RET_DK
    att_qw = ATT_Q_HEADS * ATT_HD
    n_pos = seq // tm
    tok = lambda width: pl.BlockSpec((tm, width), lambda i: (i, 0))
    tab = pl.BlockSpec((tm, LANES), lambda i: (i % n_pos, 0))
    widths = (ret_w, ret_w, ret_w, ret_w, att_qw, 4 * LANES, LANES, d, d)
    out_specs = [tok(wd) for wd in widths]
    out_shape = [jax.ShapeDtypeStruct((t, wd), BF16) for wd in widths]
    for k_out in (1, 4, 6):
        out_specs[k_out] = pl.BlockSpec((widths[k_out], tm), lambda i: (0, i))
        out_shape[k_out] = jax.ShapeDtypeStruct((widths[k_out], t), BF16)
    return pl.pallas_call(
        functools.partial(_proj_kernel, ret_w=ret_w, att_qw=att_qw),
        grid=(t // tm,),
        in_specs=[tok(d), _resident(gn.shape), _resident(w.shape), tab, tab, tab, tab],
        out_specs=out_specs,
        out_shape=out_shape,
        scratch_shapes=[pltpu.VMEM((tm, d), BF16)],
        compiler_params=_cparams(1),
        name="proj",
    )(x1, gn, w, cosr, sinr, cosa, sina)


def _ret_stages(h, dec_ref, q_ref, kt_ref, v_ref, s_ref, gain_ref, z_ref, kv_ref, st_ref):
    c = RET_TILE
    dk = kt_ref.shape[0]
    n_chunks = q_ref.shape[0] // c

    def log_gamma(direction, shape):
        return -jnp.exp(jnp.full(shape, dec_ref[direction, h], F32))

    def iota(shape, axis):
        return lax.broadcasted_iota(jnp.int32, shape, axis).astype(F32)

    rel = iota((c, c), 0) - iota((c, c), 1)
    decay = jnp.where(rel >= 0, jnp.exp(jnp.maximum(rel, 0.0) * log_gamma(0, (c, c))),
                      jnp.exp(jnp.maximum(-rel, 0.0) * log_gamma(1, (c, c))))
    pos = iota((c, dk), 0)
    kdec_f = jnp.exp((c - 1 - pos) * log_gamma(0, (c, dk)))
    kdec_b = jnp.exp(pos * log_gamma(1, (c, dk)))
    qdec_f = jnp.exp((pos + 1.0) * log_gamma(0, (c, dk)))
    qdec_b = jnp.exp((c - pos) * log_gamma(1, (c, dk)))
    g_f = jnp.exp(c * log_gamma(0, (dk, dk)))
    g_b = jnp.exp(c * log_gamma(1, (dk, dk)))

    def span(i):
        return slice(i * c, (i + 1) * c)

    def kv_stage(i):
        v = v_ref[span(i), :].astype(F32)
        rhs = jnp.concatenate([(v * kdec_f).astype(BF16), (v * kdec_b).astype(BF16)], axis=1)
        kv_ref[i] = _dot(kt_ref[:, span(i)], rhs)

    def state_stage():
        state = jnp.zeros((dk, dk), F32)
        for i in range(n_chunks):
            st_ref[i, :, :dk] = state.astype(BF16)
            state = g_f * state + kv_ref[i, :, :dk]
        state = jnp.zeros((dk, dk), F32)
        for i in reversed(range(n_chunks)):
            st_ref[i, :, dk:] = state.astype(BF16)
            state = g_b * state + kv_ref[i, :, dk:]

    def out_stage(i):
        q = q_ref[span(i), :]
        scores = _dot(q, kt_ref[:, span(i)])
        inner = _dot((scores * decay).astype(BF16), v_ref[span(i), :])
        cross = _dot(q, st_ref[i])
        y = inner + qdec_f * cross[:, :dk] + qdec_b * cross[:, dk:]
        mu = jnp.mean(y, axis=-1, keepdims=True)
        yc = y - mu
        var = jnp.mean(yc * yc, axis=-1, keepdims=True)
        yn = yc * lax.rsqrt(var + EPS) * gain_ref[...]
        z_ref[span(i), :] = (yn * s_ref[span(i), :].astype(F32)).astype(BF16)

    return ([functools.partial(kv_stage, i) for i in range(n_chunks)] + [state_stage]
            + [functools.partial(out_stage, i) for i in range(n_chunks)])


def _att_stages(tile, n_tiles, sink_ref, qt_ref, k_ref, kp_ref, kn_ref, vt_ref, vtp_ref, vtn_ref, ot_ref):
    blk = ATT_BLOCK
    n_blocks = qt_ref.shape[1] // blk
    grp = ATT_Q_HEADS // ATT_KV_HEADS
    hd = ATT_HD
    neg_inf = jnp.float32(-jnp.inf)
    kpos = lax.broadcasted_iota(jnp.int32, (blk, 2 * blk), 0)
    qpos = lax.broadcasted_iota(jnp.int32, (blk, 2 * blk), 1) % blk
    prev_bias = jnp.where(kpos >= qpos, 0.0, neg_inf)
    next_bias = jnp.where(kpos <= qpos, 0.0, neg_inf)
    first_prev_bias = prev_bias + jnp.where(tile > 0, 0.0, neg_inf)
    last_next_bias = next_bias + jnp.where(tile < n_tiles - 1, 0.0, neg_inf)
    left = lax.broadcasted_iota(jnp.int32, (1, 2 * blk), 1) < blk

    def cols(j):
        return slice(j * blk, (j + 1) * blk)

    def scores(j, g, p):
        qt = jnp.concatenate([qt_ref[cols(2 * g), cols(j)], qt_ref[cols(2 * g + 1), cols(j)]], axis=1)
        kv = cols(2 * g + p)
        kwin = jnp.concatenate([kp_ref[:, kv] if j == 0 else k_ref[cols(j - 1), kv], k_ref[cols(j), kv],
                                kn_ref[:, kv] if j == n_blocks - 1 else k_ref[cols(j + 1), kv]], axis=0)
        return _dot(kwin, qt)

    def finish(j, g, p, s):
        s_prev = s[cols(0), :] + (first_prev_bias if j == 0 else prev_bias)
        s_cur = s[cols(1), :]
        s_next = s[cols(2), :] + (last_next_bias if j == n_blocks - 1 else next_bias)
        snk = jnp.where(left, sink_ref[grp * g + p], sink_ref[grp * g + 2 + p]) * LOG2_E
        m = jnp.maximum(jnp.max(jnp.maximum(jnp.maximum(s_prev, s_cur), s_next), axis=0, keepdims=True), snk)
        et = jnp.concatenate([jnp.exp2(s_prev - m).astype(BF16), jnp.exp2(s_cur - m).astype(BF16),
                              jnp.exp2(s_next - m).astype(BF16)], axis=0)
        dims = slice(g * hd, (g + 1) * hd)
        vt = jnp.concatenate([vtp_ref[dims, :] if j == 0 else vt_ref[dims, cols(j - 1)],
                              vt_ref[dims, cols(j)],
                              vtn_ref[dims, :] if j == n_blocks - 1 else vt_ref[dims, cols(j + 1)]], axis=1)
        acc = _dot(jnp.concatenate([vt, jnp.ones((BF16_SUBLANES, 3 * blk), BF16)], axis=0), et)
        denom = acc[hd:hd + 1, :] + jnp.exp2(snk - m)
        ot = acc[:hd, :] * (1.0 / denom)
        head_a = grp * g + p
        head_b = grp * g + 2 + p
        ot_ref[head_a * hd:(head_a + 1) * hd, cols(j)] = ot[:, :blk].astype(BF16)
        ot_ref[head_b * hd:(head_b + 1) * hd, cols(j)] = ot[:, blk:].astype(BF16)

    units = [(j, g, p) for j in range(n_blocks) for g in range(ATT_KV_HEADS) for p in range(2)]
    pending = []

    def stage(n):
        if n == 0:
            pending.extend(scores(*u) for u in units[:ATT_SCORES_AHEAD])
        if n + ATT_SCORES_AHEAD < len(units):
            pending.append(scores(*units[n + ATT_SCORES_AHEAD]))
        finish(*units[n], pending.pop(0))

    return [functools.partial(stage, n) for n in range(len(units))]


def _interleave(a, b):
    if len(a) < len(b):
        a, b = b, a
    out, taken = [], 0
    for n, stage in enumerate(a):
        out.append(stage)
        due = (n + 1) * len(b) // len(a)
        out += b[taken:due]
        taken = due
    return out


def _mixer_kernel(dec_ref, sink_ref, q_ref, kt_ref, v_ref, s_ref, gain_ref,
                  qt_ref, k_ref, kp_ref, kn_ref, vt_ref, vtp_ref, vtn_ref,
                  z_ref, ot_ref, kv_ref, st_ref):
    j = pl.program_id(1)
    ret = _ret_stages(j, dec_ref, q_ref, kt_ref, v_ref, s_ref, gain_ref, z_ref, kv_ref, st_ref)
    att = _att_stages(j, pl.num_programs(1), sink_ref, qt_ref, k_ref, kp_ref, kn_ref,
                      vt_ref, vtp_ref, vtn_ref, ot_ref)
    for stage in _interleave(ret, att):
        stage()


def _mixer_call(dec, sink, qr, krt, vr, sr, gain, qat, k4, vat, batch, seq):
    t, ret_w = qr.shape
    att_w = qat.shape[0]
    tq = ATT_TILE
    n_tiles = seq // tq
    assert n_tiles == RET_HEADS, "one attention tile per retention head in each grid step"
    per_tile = tq // ATT_BLOCK
    n_halo = seq // ATT_BLOCK
    n_chunks = seq // RET_TILE

    def prev_blk(b, i):
        return b * n_halo + jnp.maximum(i * per_tile - 1, 0)

    def next_blk(b, i):
        return b * n_halo + jnp.minimum((i + 1) * per_tile, n_halo - 1)

    def dims_by_tokens(rows):
        return (pl.BlockSpec((rows, tq), lambda b, i: (0, b * n_tiles + i)),
                pl.BlockSpec((rows, ATT_BLOCK), lambda b, i: (0, prev_blk(b, i))),
                pl.BlockSpec((rows, ATT_BLOCK), lambda b, i: (0, next_blk(b, i))))

    kw = k4.shape[1]
    smem = pl.BlockSpec(memory_space=pltpu.SMEM)
    head = pl.BlockSpec((seq, RET_DK), lambda b, h: (b, h))
    return pl.pallas_call(
        _mixer_kernel,
        grid=(batch, n_tiles),
        in_specs=[smem, smem, head, pl.BlockSpec((RET_DK, seq), lambda b, h: (h, b)), head, head,
                  pl.BlockSpec((1, RET_DK), lambda b, h: (0, h)),
                  dims_by_tokens(att_w)[0],
                  pl.BlockSpec((tq, kw), lambda b, i: (b * n_tiles + i, 0)),
                  pl.BlockSpec((ATT_BLOCK, kw), lambda b, i: (prev_blk(b, i), 0)),
                  pl.BlockSpec((ATT_BLOCK, kw), lambda b, i: (next_blk(b, i), 0)),
                  *dims_by_tokens(vat.shape[0])],
        out_specs=[head, dims_by_tokens(att_w)[0]],
        out_shape=[jax.ShapeDtypeStruct((t, ret_w), BF16), jax.ShapeDtypeStruct((att_w, t), BF16)],
        scratch_shapes=[pltpu.VMEM((n_chunks, RET_DK, 2 * RET_DK), F32),
                        pltpu.VMEM((n_chunks, RET_DK, 2 * RET_DK), BF16)],
        compiler_params=_cparams(2),
        name="mixers",
    )(dec, sink, qr, krt, vr, sr, gain, qat, k4, k4, k4, vat, vat, vat)


def _tail_kernel(x_ref, zr_ref, oa_ref, sgr_ref, sga_ref, wro_ref, wao_ref, wout_ref,
                 gn_ref, wg_ref, wu_ref, wo_ref, gfin_ref, y_ref, h_ref, a_ref):
    group = h_ref.shape[0]
    groups = [slice(r * group, (r + 1) * group) for r in range(x_ref.shape[0] // group)]
    y_ret, y_att = [], []
    for rows in groups:
        y_ret.append(_dot(zr_ref[rows, :], wro_ref[...]))
        y_att.append(lax.dot_general(oa_ref[:, rows], wao_ref[...], (((0,), (0,)), ((), ())),
                                     preferred_element_type=F32))
    x2 = []
    for rows, yr, ya in zip(groups, y_ret, y_att):
        merged = sgr_ref[rows, :].astype(F32) * yr + sga_ref[rows, :].astype(F32) * ya
        x2.append(x_ref[rows, :] + _dot(merged.astype(BF16), wout_ref[...]))
    for rows, x2_rows in zip(groups, x2):
        x3 = _swiglu_residual(x2_rows, gn_ref, wg_ref, wu_ref, wo_ref, h_ref, a_ref)
        y_ref[rows, :] = _rms(x3, gfin_ref[...])


def _tail_call(x1, zr, oa, sgr, sga, wro, wao, wout, gn, wg, wu, wo, gfin):
    t, d = x1.shape
    tm = TAIL_TILE
    ffp = wo.shape[0]
    tok = lambda width: pl.BlockSpec((tm, width), lambda i: (i, 0))
    return pl.pallas_call(
        _tail_kernel,
        grid=(t // tm,),
        in_specs=[tok(d), tok(zr.shape[1]), pl.BlockSpec((oa.shape[0], tm), lambda i: (0, i)), tok(d), tok(d),
                  _resident(wro.shape), _resident(wao.shape), _resident(wout.shape),
                  _resident(gn.shape), _resident(wg.shape), _resident(wu.shape), _resident(wo.shape),
                  _resident(gfin.shape)],
        out_specs=tok(d),
        out_shape=jax.ShapeDtypeStruct((t, d), F32),
        scratch_shapes=[pltpu.VMEM((TAIL_FFN_ROWS, d), BF16), pltpu.VMEM((TAIL_FFN_ROWS, ffp), BF16)],
        compiler_params=_cparams(1),
        name="tail",
    )(x1, zr, oa, sgr, sga, wro, wao, wout, gn, wg, wu, wo, gfin)


def _rope_tables(seq):
    pos = np.arange(seq, dtype=np.float32)[:, None]

    def cs(half):
        inv_freq = np.float32(ROPE_THETA) ** (-np.arange(half, dtype=np.float32) / np.float32(half))
        ang = (pos * inv_freq[None, :].astype(np.float32)).astype(np.float64)
        return np.cos(ang).astype(np.float32), np.sin(ang).astype(np.float32)

    c, s = cs(RET_DK // 2)
    cosr = np.concatenate([c, c], axis=1)
    sinr = np.concatenate([-s, s], axis=1)
    c, s = cs(ATT_HD // 2)
    cosa = np.concatenate([c, c, c, c], axis=1)
    sina = np.concatenate([-s, s, -s, s], axis=1)
    return tuple(jnp.asarray(tab) for tab in (cosr, sinr, cosa, sina))


def kernel(x, norm_ffn1, ffn1_w_in, ffn1_w_out, norm_mix, w_in, ret_decay_fwd, ret_decay_bwd,
           ret_gn_gain, w_ret_out, att_sink, w_att_out, w_out, norm_ffn2, ffn2_w_in, ffn2_w_out,
           norm_final):
    b, s, d = x.shape
    assert ffn1_w_in.shape[0] == 1, "single-layer block"
    assert all(s % tile == 0 for tile in (TAIL_TILE, FFN_TILE, PROJ_TILE, RET_TILE, ATT_TILE))
    assert TAIL_TILE % TAIL_FFN_ROWS == 0
    xt = x.reshape(b * s, d)
    cosr, sinr, cosa, sina = _rope_tables(s)

    x1, ffn2_w, (w_in_bf, w_ro_bf, w_ao_bf, w_out_bf) = _ffn_call(
        xt, norm_ffn1, ffn1_w_in[0], ffn1_w_out[0], ffn2_w_in[0], ffn2_w_out[0],
        (w_in[0], w_ret_out[0], w_att_out[0], w_out[0]))
    qr, krt, vr, sr, qat, k4, vat, sgr, sga = _proj_call(
        x1, norm_mix, w_in_bf, cosr, sinr, cosa, sina, s)
    dec = jnp.concatenate([ret_decay_fwd, ret_decay_bwd], axis=0)
    zr, oat = _mixer_call(dec, att_sink[0], qr, krt, vr, sr, ret_gn_gain, qat, k4, vat, b, s)
    y = _tail_call(x1, zr, oat, sgr, sga, w_ro_bf, w_ao_bf, w_out_bf, norm_ffn2, *ffn2_w,
                   norm_final.reshape(1, d))
    return y.reshape(b, s, d)
```
